```python
import math
import jax, jax.numpy as jnp
from jax import lax
import numpy as np

D_MODEL = 2048
BATCH = 2
SEQ = 4096
DEPTH = 4
DEC_BATCH = 128
DEC_SEQ = 1
PAST_LEN = 8192
PAGE_SIZE = 128

CONV_DIM = D_MODEL // 4
POOL_DIM = D_MODEL // 4
N_HEADS = 8
HEAD_DIM_NOPE = 128
ROPE_DIM = 64
HEAD_DIM_QK = HEAD_DIM_NOPE + ROPE_DIM
HEAD_DIM_V = 128
ATTN_DIM = N_HEADS * HEAD_DIM_V
MIX_DIM = CONV_DIM + POOL_DIM + ATTN_DIM
Q_RANK = D_MODEL // 4
KV_RANK = D_MODEL // 8
CONV_WIDTH = 3
POOL_WINDOWS = (2, 4, 8, 16)
N_POOL_GROUPS = len(POOL_WINDOWS)
POOL_GROUP = POOL_DIM // N_POOL_GROUPS
POOL_CTX = max(POOL_WINDOWS) - 1
D_FF = -(-(8 * D_MODEL) // (3 * 256)) * 256
IN_SPLITS = (CONV_DIM, CONV_DIM, CONV_DIM, POOL_DIM, Q_RANK, KV_RANK, ROPE_DIM)
IN_DIM = sum(IN_SPLITS)
ROPE_THETA = 10000.0
EPS = 1e-6
Q_BLOCK = 128
QK_SCALE = HEAD_DIM_QK ** -0.5

kernel_name = "hybrid_conv_pool_mla_decoder_step"


def rms_norm(x, g):
    xf = x.astype(jnp.float32)
    y = xf * lax.rsqrt(jnp.mean(xf * xf, axis=-1, keepdims=True) + EPS)
    return (y * g.astype(jnp.float32)).astype(x.dtype)


def rope(x, pos):
    half = ROPE_DIM // 2
    inv = ROPE_THETA ** (-jnp.arange(half, dtype=jnp.float32) / half)
    ang = pos.astype(jnp.float32)[:, None] * inv[None, :]
    cos = jnp.cos(ang)[None, :, None, :]
    sin = jnp.sin(ang)[None, :, None, :]
    xf = x.astype(jnp.float32)
    x1, x2 = xf[..., :half], xf[..., half:]
    return jnp.concatenate([x1 * cos - x2 * sin, x1 * sin + x2 * cos], axis=-1).astype(x.dtype)


def mla_queries(cq, pos, w_uq, g_q):
    q = jnp.einsum('btr,rhd->bthd', cq, w_uq)
    q = rms_norm(q, g_q)
    return jnp.concatenate([q[..., :HEAD_DIM_NOPE], rope(q[..., HEAD_DIM_NOPE:], pos)], axis=-1)


def mla_keys(ckv, kr, pos, w_uk, g_k):
    k_nope = jnp.einsum('btr,rhd->bthd', ckv, w_uk)
    k_rope = jnp.broadcast_to(kr[:, :, None, :], k_nope.shape[:3] + (ROPE_DIM,))
    k = rms_norm(jnp.concatenate([k_nope, k_rope], axis=-1), g_k)
    return jnp.concatenate([k[..., :HEAD_DIM_NOPE], rope(k[..., HEAD_DIM_NOPE:], pos)], axis=-1)


def mla_prompt(q, ckv, kr, pos, w_uk, w_uv, g_k):
    n_b, n_t = q.shape[:2]
    k = mla_keys(ckv, kr, pos, w_uk, g_k)
    n_blk = n_t // Q_BLOCK
    qb = q.reshape(n_b, n_blk, Q_BLOCK, N_HEADS, HEAD_DIM_QK).transpose(1, 0, 2, 3, 4)
    kpos = jnp.arange(n_t)

    def one_block(args):
        q_blk, i = args
        s = jnp.einsum('bqhd,bkhd->bhqk', q_blk, k).astype(jnp.float32) * QK_SCALE
        qpos = i * Q_BLOCK + jnp.arange(Q_BLOCK)
        s = jnp.where(kpos[None, :] <= qpos[:, None], s, -jnp.inf)
        p = jax.nn.softmax(s, axis=-1).astype(ckv.dtype)
        return jnp.einsum('bhqk,bkr->bqhr', p, ckv)

    lat = lax.map(one_block, (qb, jnp.arange(n_blk)))
    lat = lat.transpose(1, 0, 2, 3, 4).reshape(n_b, n_t, N_HEADS, KV_RANK)
    out = jnp.einsum('bthr,rhd->bthd', lat, w_uv)
    return out.reshape(n_b, n_t, ATTN_DIM)


def online_update(m, l, acc, s, c):
    m_new = jnp.maximum(m, jnp.max(s, axis=-1))
    corr = jnp.exp(m - m_new)
    p = jnp.exp(s - m_new[..., None])
    l = l * corr + jnp.sum(p, axis=-1)
    acc = acc * corr[..., None] + jnp.einsum('bhqk,bkr->bhqr', p, c.astype(jnp.float32))
    return (m_new, l, acc)


def mla_sample(q, ckv_new, kr_new, pos_new, cache_ckv, cache_kr, layer, page_table, w_uk, w_uv, g_k):
    n_b, n_t = q.shape[:2]
    n_pages = page_table.shape[1]
    f32 = jnp.float32

    def page_step(carry, inp):
        m, l, acc = carry
        phys, page = inp
        c_blk = cache_ckv[layer, phys]
        r_blk = cache_kr[layer, phys]
        kpos = page * PAGE_SIZE + jnp.arange(PAGE_SIZE)
        k = mla_keys(c_blk, r_blk, kpos, w_uk, g_k)
        s = jnp.einsum('bqhd,bkhd->bhqk', q, k).astype(f32) * QK_SCALE
        return online_update(m, l, acc, s, c_blk), None

    init = (jnp.full((n_b, N_HEADS, n_t), -jnp.inf, f32),
            jnp.zeros((n_b, N_HEADS, n_t), f32),
            jnp.zeros((n_b, N_HEADS, n_t, KV_RANK), f32))
    (m, l, acc), _ = lax.scan(page_step, init, (page_table.T, jnp.arange(n_pages)))
    k_new = mla_keys(ckv_new, kr_new, pos_new, w_uk, g_k)
    s = jnp.einsum('bqhd,bkhd->bhqk', q, k_new).astype(f32) * QK_SCALE
    causal = jnp.tril(jnp.ones((n_t, n_t), dtype=bool))
    s = jnp.where(causal, s, -jnp.inf)
    m, l, acc = online_update(m, l, acc, s, ckv_new)
    lat = (acc / l[..., None]).astype(q.dtype)
    out = jnp.einsum('bhtr,rhd->bthd', lat, w_uv)
    return out.reshape(n_b, n_t, ATTN_DIM)


def short_conv(u_ext, w_conv):
    n_t = u_ext.shape[1] - (CONV_WIDTH - 1)
    return sum(w_conv[j] * u_ext[:, j:j + n_t] for j in range(CONV_WIDTH))


def multi_pool(v_ext, pos, w_pool, s_pool):
    n_b = v_ext.shape[0]
    n_t = pos.shape[0]
    cs = jnp.cumsum(v_ext.astype(jnp.float32), axis=1)
    cs = jnp.concatenate([jnp.zeros((n_b, 1, POOL_DIM), jnp.float32), cs], axis=1)
    v = v_ext[:, POOL_CTX:]
    end = cs[:, POOL_CTX + 1:POOL_CTX + 1 + n_t]
    outs = []
    for g, w in enumerate(POOL_WINDOWS):
        sl = slice(g * POOL_GROUP, (g + 1) * POOL_GROUP)
        start = cs[:, POOL_CTX + 1 - w:POOL_CTX + 1 - w + n_t, sl]
        cnt = jnp.minimum(pos + 1, w).astype(jnp.float32)[None, :, None]
        mean = (end[..., sl] - start) / cnt
        outs.append(mean.astype(v.dtype) - v[..., sl])
    d = jnp.stack(outs, axis=2)
    y = jnp.einsum('btgc,gcd->btgd', d, w_pool).reshape(n_b, n_t, POOL_DIM)
    return y * s_pool


def trunk_layer(x, c, pos, conv_prev, pool_prev, attend, p):
    mod = jnp.einsum('bd,de->be', jax.nn.silu(c), p['w_ada']) + p['b_ada']
    sh1, sc1, g1, sh2, sc2, g2 = jnp.split(mod[:, None, :], 6, axis=-1)
    h = rms_norm(x, p['g_mix']) * (1 + sc1) + sh1
    z = jnp.einsum('btd,de->bte', h, p['w_in'])
    idx = list(np.cumsum(IN_SPLITS)[:-1])
    hc, bc, cc, vp, cq, ckv, kr = jnp.split(z, idx, axis=-1)
    u_ext = jnp.concatenate([conv_prev, cc * hc], axis=1)
    y_conv = bc * short_conv(u_ext, p['w_conv'])
    v_ext = jnp.concatenate([pool_prev, vp], axis=1)
    y_pool = multi_pool(v_ext, pos, p['w_pool'], p['s_pool'])
    cq = rms_norm(cq, p['g_cq'])
    ckv = rms_norm(ckv, p['g_ckv'])
    q = mla_queries(cq, pos, p['w_uq'], p['g_q'])
    y_attn = attend(q, ckv, kr, pos, p)
    y = jnp.einsum('bte,ed->btd', jnp.concatenate([y_conv, y_pool, y_attn], axis=-1), p['w_o'])
    x = x + g1 * y
    h = rms_norm(x, p['g_ffn']) * (1 + sc2) + sh2
    f = jax.nn.silu(jnp.einsum('btd,df->btf', h, p['w1'])) * jnp.einsum('btd,df->btf', h, p['w3'])
    x = x + g2 * jnp.einsum('btf,fd->btd', f, p['w2'])
    return x, u_ext[:, -(CONV_WIDTH - 1):], v_ext[:, -POOL_CTX:], ckv, kr


def setup_inputs(seed: int = 0) -> dict:
    key = jax.random.key(seed)
    ks = jax.random.split(key, 32)
    f32 = jnp.float32
    n_pages = PAST_LEN // PAGE_SIZE
    n_used = DEC_BATCH * n_pages
    n_pool = n_used + (n_used + 3) // 4

    def nrm(k, shape, scale=1.0):
        return jax.random.normal(k, shape, f32) * scale

    def gain(k, shape):
        return 1.0 + 0.02 * jax.random.normal(k, shape, f32)

    page_table = jax.random.permutation(ks[6], n_pool)[:n_used].reshape(DEC_BATCH, n_pages).astype(jnp.int32)
    return {
        'x_prompt': nrm(ks[0], (BATCH, SEQ, D_MODEL)),
        'x_sample': nrm(ks[1], (DEC_BATCH, DEC_SEQ, D_MODEL)),
        'cache_ckv': nrm(ks[2], (DEPTH, n_pool, PAGE_SIZE, KV_RANK)),
        'cache_kr': nrm(ks[3], (DEPTH, n_pool, PAGE_SIZE, ROPE_DIM)),
        'state_conv': nrm(ks[4], (DEPTH, DEC_BATCH, CONV_WIDTH - 1, CONV_DIM)),
        'state_pool': nrm(ks[5], (DEPTH, DEC_BATCH, POOL_CTX, POOL_DIM)),
        'page_table': page_table,
        'c_prompt': nrm(ks[7], (BATCH, D_MODEL)),
        'c_sample': nrm(ks[8], (DEC_BATCH, D_MODEL)),
        'g_mix': gain(ks[9], (DEPTH, D_MODEL)),
        'w_ada': nrm(ks[10], (DEPTH, D_MODEL, 6 * D_MODEL), 0.5 * D_MODEL ** -0.5),
        'b_ada': nrm(ks[11], (DEPTH, 6 * D_MODEL), 0.01),
        'w_in': nrm(ks[12], (DEPTH, D_MODEL, IN_DIM), D_MODEL ** -0.5),
        'w_conv': nrm(ks[13], (DEPTH, CONV_WIDTH, CONV_DIM), CONV_WIDTH ** -0.5),
        'w_pool': nrm(ks[14], (DEPTH, N_POOL_GROUPS, POOL_GROUP, POOL_GROUP), POOL_GROUP ** -0.5),
        's_pool': gain(ks[15], (DEPTH, POOL_DIM)),
        'g_cq': gain(ks[16], (DEPTH, Q_RANK)),
        'g_ckv': gain(ks[17], (DEPTH, KV_RANK)),
        'w_uq': nrm(ks[18], (DEPTH, Q_RANK, N_HEADS, HEAD_DIM_QK), Q_RANK ** -0.5),
        'w_uk': nrm(ks[19], (DEPTH, KV_RANK, N_HEADS, HEAD_DIM_NOPE), KV_RANK ** -0.5),
        'w_uv': nrm(ks[20], (DEPTH, KV_RANK, N_HEADS, HEAD_DIM_V), KV_RANK ** -0.5),
        'g_q': gain(ks[21], (DEPTH, HEAD_DIM_QK)),
        'g_k': gain(ks[22], (DEPTH, HEAD_DIM_QK)),
        'w_o': nrm(ks[23], (DEPTH, MIX_DIM, D_MODEL), MIX_DIM ** -0.5),
        'g_ffn': gain(ks[24], (DEPTH, D_MODEL)),
        'w1': nrm(ks[25], (DEPTH, D_MODEL, D_FF), D_MODEL ** -0.5),
        'w3': nrm(ks[26], (DEPTH, D_MODEL, D_FF), D_MODEL ** -0.5),
        'w2': nrm(ks[27], (DEPTH, D_FF, D_MODEL), D_FF ** -0.5),
    }


def reference(x_prompt, x_sample, cache_ckv, cache_kr, state_conv, state_pool, page_table,
              c_prompt, c_sample, g_mix, w_ada, b_ada, w_in, w_conv, w_pool, s_pool,
              g_cq, g_ckv, w_uq, w_uk, w_uv, g_q, g_k, w_o, g_ffn, w1, w3, w2):
    n_bp, n_tp = x_prompt.shape[:2]
    n_ts = x_sample.shape[1]
    past = page_table.shape[1] * PAGE_SIZE
    pos_p = jnp.arange(n_tp)
    pos_s = past + jnp.arange(n_ts)
    conv_zero = jnp.zeros((n_bp, CONV_WIDTH - 1, CONV_DIM), x_prompt.dtype)
    pool_zero = jnp.zeros((n_bp, POOL_CTX, POOL_DIM), x_prompt.dtype)

    def prompt_attend(q, ckv, kr, pos, p):
        return mla_prompt(q, ckv, kr, pos, p['w_uk'], p['w_uv'], p['g_k'])

    xp, xs = x_prompt, x_sample
    ckv_p, kr_p, conv_p, pool_p = [], [], [], []
    ckv_s, kr_s, conv_s, pool_s = [], [], [], []
    for l in range(DEPTH):
        p = {'g_mix': g_mix[l], 'w_ada': w_ada[l], 'b_ada': b_ada[l], 'w_in': w_in[l],
             'w_conv': w_conv[l], 'w_pool': w_pool[l], 's_pool': s_pool[l], 'g_cq': g_cq[l],
             'g_ckv': g_ckv[l], 'w_uq': w_uq[l], 'w_uk': w_uk[l], 'w_uv': w_uv[l], 'g_q': g_q[l],
             'g_k': g_k[l], 'w_o': w_o[l], 'g_ffn': g_ffn[l], 'w1': w1[l], 'w3': w3[l], 'w2': w2[l]}

        def sample_attend(q, ckv, kr, pos, p, layer=l):
            return mla_sample(q, ckv, kr, pos, cache_ckv, cache_kr, layer, page_table,
                              p['w_uk'], p['w_uv'], p['g_k'])

        xp, cv, pl, ck, kr = trunk_layer(xp, c_prompt, pos_p, conv_zero, pool_zero, prompt_attend, p)
        ckv_p.append(ck); kr_p.append(kr); conv_p.append(cv); pool_p.append(pl)
        xs, cv, pl, ck, kr = trunk_layer(xs, c_sample, pos_s, state_conv[l], state_pool[l], sample_attend, p)
        ckv_s.append(ck); kr_s.append(kr); conv_s.append(cv); pool_s.append(pl)

    return (xp, xs,
            jnp.stack(ckv_p), jnp.stack(kr_p), jnp.stack(conv_p), jnp.stack(pool_p),
            jnp.stack(ckv_s), jnp.stack(kr_s), jnp.stack(conv_s), jnp.stack(pool_s))
```

```python
import functools
import math

import jax
import jax.numpy as jnp
from jax import lax
from jax.experimental import pallas as pl
from jax.experimental.pallas import tpu as pltpu

F32 = jnp.float32
BF16 = jnp.bfloat16

EPS = 1e-6
ROPE_THETA = 10000.0
POOL_WINDOWS = (2, 4, 8, 16)

LANES = 128
SUBLANES = 8
VMEM_BYTES_V7X = 64 * 1024 * 1024
VMEM_LIMIT = VMEM_BYTES_V7X - 8 * 1024 * 1024

ROPE_SPREAD = LANES
ROPE_HALF_OFF = LANES // 2


def _params(semantics):
    return pltpu.CompilerParams(dimension_semantics=semantics, vmem_limit_bytes=VMEM_LIMIT)


def _resident(block_shape, index_map):
    return pl.BlockSpec(block_shape, index_map, pipeline_mode=pl.Buffered(1))


def _rms(x, g):
    return x * lax.rsqrt(jnp.mean(x * x, axis=-1, keepdims=True) + EPS) * g


def _nt_dot(a, b):
    return lax.dot_general(a, b, (((1,), (1,)), ((), ())), preferred_element_type=F32)


def _rope_spread(x, cc, ss):
    return x * cc + pltpu.roll(x, ROPE_HALF_OFF, 1) * ss


def _ada_kernel(c_ref, w_ref, b_ref, o_ref):
    c = c_ref[...]
    a = (c * jax.nn.sigmoid(c)).astype(BF16)
    o_ref[...] = jnp.dot(a, w_ref[...].astype(BF16), preferred_element_type=F32) + b_ref[...]


def _ada_mod(c_all, w_ada, b_ada, tn):
    n_layers, d_model, six_d = w_ada.shape
    rows = c_all.shape[0]
    n_per = d_model // tn
    return pl.pallas_call(
        _ada_kernel,
        grid=(n_layers, 6, n_per),
        in_specs=[
            pl.BlockSpec((rows, d_model), lambda l, k, n: (0, 0)),
            pl.BlockSpec((None, d_model, tn), lambda l, k, n: (l, 0, k * n_per + n)),
            pl.BlockSpec((None, 1, tn), lambda l, k, n: (l, 0, k * n_per + n)),
        ],
        out_specs=pl.BlockSpec((None, None, rows, tn), lambda l, k, n: (l, k, 0, n)),
        out_shape=jax.ShapeDtypeStruct((n_layers, 6, rows, d_model), F32),
        compiler_params=_params(("arbitrary", "arbitrary", "arbitrary")),
        name="ada_mod",
    )(c_all, w_ada, b_ada.reshape(n_layers, 1, six_d))


def _mod_spec(mod, layer, chunk, tm, rows_per_batch):
    if mod.ndim == 5:
        tiles_per_batch = rows_per_batch // tm
        return pl.BlockSpec((None, None, None, 1, mod.shape[-1]),
                            lambda i: (layer, chunk, i // tiles_per_batch, 0, 0))
    return pl.BlockSpec((None, None, tm, mod.shape[-1]), lambda i: (layer, chunk, i, 0))


def _inproj_kernel(x_ref, sh_ref, sc_ref, g_ref, w_ref, z_ref):
    h = _rms(x_ref[...], g_ref[...]) * (1.0 + sc_ref[...]) + sh_ref[...]
    z_ref[...] = jnp.dot(h.astype(BF16), w_ref[...], preferred_element_type=F32)


def _inproj(x, mod, g_mix, w_in_p, layer, tm, rows_per_batch):
    m, d_model = x.shape
    n_out = w_in_p.shape[-1]
    return pl.pallas_call(
        _inproj_kernel,
        grid=(m // tm,),
        in_specs=[
            pl.BlockSpec((tm, d_model), lambda i: (i, 0)),
            _mod_spec(mod, layer, 0, tm, rows_per_batch),
            _mod_spec(mod, layer, 1, tm, rows_per_batch),
            pl.BlockSpec((None, 1, d_model), lambda i: (layer, 0, 0)),
            _resident((None, d_model, n_out), lambda i: (layer, 0, 0)),
        ],
        out_specs=pl.BlockSpec((tm, n_out), lambda i: (i, 0)),
        out_shape=jax.ShapeDtypeStruct((m, n_out), F32),
        compiler_params=_params(("arbitrary",)),
        name="inproj",
    )(x, mod, mod, g_mix, w_in_p)


def _queries(cq, gcq, wuq, gq, cc, ss, n_heads, dn, dqk, scale):
    cqn = _rms(cq, gcq).astype(BF16)
    qall = jnp.dot(cqn, wuq, preferred_element_type=F32)
    hw = dn + ROPE_SPREAD
    out = []
    for h in range(n_heads):
        qn = qall[:, h * hw:h * hw + dn]
        qr = qall[:, h * hw + dn:(h + 1) * hw]
        ssq = jnp.sum(qn * qn, axis=-1, keepdims=True) + jnp.sum(qr * qr, axis=-1, keepdims=True)
        r = lax.rsqrt(ssq / dqk + EPS) * scale
        qn = qn * r * gq[:, :dn]
        qr = _rope_spread(qr * r * gq[:, dn:], cc, ss)
        out.append((qn, qr))
    return out


def _keys(ckvn_b, kr, wuk, gkn, gkr, cc, ss, n_heads, dn, dqk):
    kn_all = jnp.dot(ckvn_b, wuk, preferred_element_type=F32)
    krsq = jnp.sum(kr * kr, axis=-1, keepdims=True)
    krot = _rope_spread(kr * gkr, cc, ss)
    out = []
    for h in range(n_heads):
        kn = kn_all[:, h * dn:(h + 1) * dn]
        r = lax.rsqrt((jnp.sum(kn * kn, axis=-1, keepdims=True) + krsq) / dqk + EPS)
        out.append((kn * r * gkn, krot * r))
    return out


def _mix_prompt_kernel(z_ref, cc_ref, ss_ref, wconv_ref, wpool_ref, spool_ref, gcq_ref, gckv_ref,
                       wuq_ref, wuk_ref, gq_ref, gkn_ref, gkr_ref,
                       mix_ref, q_ref, k_ref, ckv_ref, vb_ref, kr_ref, cst_ref, pst_ref,
                       ubuf, vbuf, *, dims):
    (tc, conv_dim, pool_dim, q_rank, kv_rank, n_heads, dn, dqk, scale, conv_w, pool_ctx) = dims
    j = pl.program_id(1)
    halo_u = SUBLANES
    halo_v = 2 * SUBLANES

    @pl.when(j == 0)
    def _():
        ubuf[0:halo_u, :] = jnp.zeros((halo_u, conv_dim), F32)
        vbuf[0:halo_v, :] = jnp.zeros((halo_v, pool_dim), F32)

    o = 0
    hc = z_ref[:, o:o + conv_dim]; o += conv_dim
    bc = z_ref[:, o:o + conv_dim]; o += conv_dim
    cg = z_ref[:, o:o + conv_dim]; o += conv_dim
    vp = z_ref[:, o:o + pool_dim]; o += pool_dim
    cq = z_ref[:, o:o + q_rank]; o += q_rank
    ckv = z_ref[:, o:o + kv_rank]; o += kv_rank
    kr = z_ref[:, o:o + ROPE_SPREAD]

    u = cg * hc
    ubuf[halo_u:halo_u + tc, :] = u
    acc = wconv_ref[conv_w - 1:conv_w, :] * u
    for t in range(1, conv_w):
        acc = acc + wconv_ref[conv_w - 1 - t:conv_w - t, :] * ubuf[pl.ds(halo_u - t, tc), :]
    mix_ref[:, 0:conv_dim] = (bc * acc).astype(BF16)
    cst_ref[...] = ubuf[pl.ds(halo_u + tc - (conv_w - 1), conv_w - 1), :]
    ubuf[0:halo_u, :] = ubuf[tc:tc + halo_u, :]

    vbuf[halo_v:halo_v + tc, :] = vp
    pg = pool_dim // len(POOL_WINDOWS)
    pos = j * tc + lax.broadcasted_iota(jnp.int32, (tc, pg), 0)
    for g, w in enumerate(POOL_WINDOWS):
        sl = slice(g * pg, (g + 1) * pg)
        v = vp[:, sl]
        s = v
        for t in range(1, w):
            s = s + vbuf[pl.ds(halo_v - t, tc), sl]
        cnt = jnp.minimum(pos + 1, w).astype(F32)
        d = s / cnt - v
        y = jnp.dot(d.astype(BF16), wpool_ref[g], preferred_element_type=F32) * spool_ref[:, sl]
        mix_ref[:, conv_dim + g * pg:conv_dim + (g + 1) * pg] = y.astype(BF16)
    pst_ref[...] = vbuf[pl.ds(halo_v + tc - pool_ctx, pool_ctx), :]
    vbuf[0:halo_v, :] = vbuf[tc:tc + halo_v, :]

    cc = cc_ref[...]
    ss = ss_ref[...]
    qs = _queries(cq, gcq_ref[...], wuq_ref[...], gq_ref[...], cc, ss, n_heads, dn, dqk, scale)
    for h, (qn, qr) in enumerate(qs):
        q_ref[h, :, 0:dn] = qn.astype(BF16)
        q_ref[h, :, dn:dn + ROPE_SPREAD] = qr.astype(BF16)
    ckvn = _rms(ckv, gckv_ref[...])
    ckv_ref[...] = ckvn
    ckvn_b = ckvn.astype(BF16)
    vb_ref[...] = ckvn_b
    kr_ref[...] = kr
    ks = _keys(ckvn_b, kr, wuk_ref[...], gkn_ref[...], gkr_ref[...], cc, ss, n_heads, dn, dqk)
    for h, (kn, krot) in enumerate(ks):
        k_ref[h, :, 0:dn] = kn.astype(BF16)
        k_ref[h, :, dn:dn + ROPE_SPREAD] = krot.astype(BF16)


def _mix_prompt(z, cc, ss, lw, layer, n_batch, n_tok, tc, cfg):
    (conv_dim, pool_dim, q_rank, kv_rank, n_heads, dn, dqk, conv_w, pool_ctx) = cfg
    n_in = z.shape[-1]
    hw = dn + ROPE_SPREAD
    tiles = n_tok // tc
    dims = (tc, conv_dim, pool_dim, q_rank, kv_rank, n_heads, dn, dqk, dqk ** -0.5, conv_w, pool_ctx)
    lay = lambda *shape: pl.BlockSpec((None,) + shape, lambda b, j: (layer,) + (0,) * len(shape))
    n_groups = len(POOL_WINDOWS)
    pg = pool_dim // n_groups
    mix_dim = conv_dim + pool_dim
    return pl.pallas_call(
        functools.partial(_mix_prompt_kernel, dims=dims),
        grid=(n_batch, tiles),
        in_specs=[
            pl.BlockSpec((tc, n_in), lambda b, j: (b * tiles + j, 0)),
            pl.BlockSpec((tc, ROPE_SPREAD), lambda b, j: (j, 0)),
            pl.BlockSpec((tc, ROPE_SPREAD), lambda b, j: (j, 0)),
            lay(conv_w, conv_dim),
            lay(n_groups, pg, pg),
            lay(1, pool_dim),
            lay(1, q_rank),
            lay(1, kv_rank),
            lay(q_rank, n_heads * hw),
            lay(kv_rank, n_heads * dn),
            lay(1, hw),
            lay(1, dn),
            lay(1, ROPE_SPREAD),
        ],
        out_specs=[
            pl.BlockSpec((tc, mix_dim), lambda b, j: (b * tiles + j, 0)),
            pl.BlockSpec((None, n_heads, tc, hw), lambda b, j: (b, 0, j, 0)),
            pl.BlockSpec((None, n_heads, tc, hw), lambda b, j: (b, 0, j, 0)),
            pl.BlockSpec((None, tc, kv_rank), lambda b, j: (b, j, 0)),
            pl.BlockSpec((None, tc, kv_rank), lambda b, j: (b, j, 0)),
            pl.BlockSpec((None, tc, ROPE_SPREAD), lambda b, j: (b, j, 0)),
            pl.BlockSpec((None, conv_w - 1, conv_dim), lambda b, j: (b, 0, 0)),
            pl.BlockSpec((None, pool_ctx, pool_dim), lambda b, j: (b, 0, 0)),
        ],
        out_shape=[
            jax.ShapeDtypeStruct((n_batch * n_tok, mix_dim), BF16),
            jax.ShapeDtypeStruct((n_batch, n_heads, n_tok, hw), BF16),
            jax.ShapeDtypeStruct((n_batch, n_heads, n_tok, hw), BF16),
            jax.ShapeDtypeStruct((n_batch, n_tok, kv_rank), F32),
            jax.ShapeDtypeStruct((n_batch, n_tok, kv_rank), BF16),
            jax.ShapeDtypeStruct((n_batch, n_tok, ROPE_SPREAD), F32),
            jax.ShapeDtypeStruct((n_batch, conv_w - 1, conv_dim), F32),
            jax.ShapeDtypeStruct((n_batch, pool_ctx, pool_dim), F32),
        ],
        scratch_shapes=[
            pltpu.VMEM((SUBLANES + tc, conv_dim), F32),
            pltpu.VMEM((2 * SUBLANES + tc, pool_dim), F32),
        ],
        compiler_params=_params(("arbitrary", "arbitrary")),
        name="mix_prompt",
    )(z, cc, ss, lw["w_conv"], lw["w_pool"], lw["s_pool"], lw["g_cq"], lw["g_ckv"],
      lw["w_uq"], lw["w_uk"], lw["g_q"], lw["g_kn"], lw["g_kr"])


def _mix_sample_kernel(z_ref, cst_ref, pst_ref, cc_ref, ss_ref, wconv_ref, wpool_ref, spool_ref,
                       gcq_ref, gckv_ref, wuq_ref, wuk_ref, gq_ref, gkn_ref, gkr_ref, gkrb_ref,
                       mix_ref, u_ref, ckv_ref, qt_ref, qa_ref, qb_ref, sself_ref, *, dims):
    (conv_dim, pool_dim, q_rank, kv_rank, n_heads, dn, dqk, scale, conv_w, pool_ctx, past) = dims
    o = 0
    hc = z_ref[:, o:o + conv_dim]; o += conv_dim
    bc = z_ref[:, o:o + conv_dim]; o += conv_dim
    cg = z_ref[:, o:o + conv_dim]; o += conv_dim
    vp = z_ref[:, o:o + pool_dim]; o += pool_dim
    cq = z_ref[:, o:o + q_rank]; o += q_rank
    ckv = z_ref[:, o:o + kv_rank]; o += kv_rank
    kr = z_ref[:, o:o + ROPE_SPREAD]

    u = cg * hc
    u_ref[...] = u
    acc = wconv_ref[conv_w - 1:conv_w, :] * u
    for t in range(1, conv_w):
        acc = acc + wconv_ref[conv_w - 1 - t:conv_w - t, :] * cst_ref[conv_w - 1 - t]
    mix_ref[:, 0:conv_dim] = (bc * acc).astype(BF16)

    pg = pool_dim // len(POOL_WINDOWS)
    for g, w in enumerate(POOL_WINDOWS):
        sl = slice(g * pg, (g + 1) * pg)
        v = vp[:, sl]
        s = v
        for t in range(1, w):
            s = s + pst_ref[pool_ctx - t, :, sl]
        d = s / float(min(past + 1, w)) - v
        y = jnp.dot(d.astype(BF16), wpool_ref[g], preferred_element_type=F32) * spool_ref[:, sl]
        mix_ref[:, conv_dim + g * pg:conv_dim + (g + 1) * pg] = y.astype(BF16)

    cc = cc_ref[...]
    ss = ss_ref[...]
    qs = _queries(cq, gcq_ref[...], wuq_ref[...], gq_ref[...], cc, ss, n_heads, dn, dqk, scale)
    ckvn = _rms(ckv, gckv_ref[...])
    ckv_ref[...] = ckvn
    wuk = wuk_ref[...]
    gkn = gkn_ref[...]
    ks = _keys(ckvn.astype(BF16), kr, wuk, gkn, gkr_ref[...], cc, ss, n_heads, dn, dqk)
    for h in range(n_heads):
        qn, qr = qs[h]
        kn, krot = ks[h]
        sself_ref[:, h:h + 1] = (jnp.sum(qn * kn, axis=-1, keepdims=True)
                                 + jnp.sum(qr * krot, axis=-1, keepdims=True))
        qt_ref[h] = _nt_dot((qn * gkn).astype(BF16), wuk[:, h * dn:(h + 1) * dn])
        qa_ref[h] = qr * gkr_ref[...]
        qb_ref[h] = pltpu.roll(qr, ROPE_HALF_OFF, 1) * gkrb_ref[...]


def _mix_sample(z, cst, pst, cc, ss, lw, layer, cfg, past):
    (conv_dim, pool_dim, q_rank, kv_rank, n_heads, dn, dqk, conv_w, pool_ctx) = cfg
    rows, n_in = z.shape
    hw = dn + ROPE_SPREAD
    dims = (conv_dim, pool_dim, q_rank, kv_rank, n_heads, dn, dqk, dqk ** -0.5, conv_w, pool_ctx, past)
    lay = lambda *shape: pl.BlockSpec((None,) + shape, lambda i: (layer,) + (0,) * len(shape))
    full = lambda *shape: pl.BlockSpec(shape, lambda i: (0,) * len(shape))
    n_groups = len(POOL_WINDOWS)
    pg = pool_dim // n_groups
    mix_dim = conv_dim + pool_dim
    return pl.pallas_call(
        functools.partial(_mix_sample_kernel, dims=dims),
        grid=(1,),
        in_specs=[
            full(rows, n_in),
            lay(conv_w - 1, rows, conv_dim),
            lay(pool_ctx, rows, pool_dim),
            full(1, ROPE_SPREAD),
            full(1, ROPE_SPREAD),
            lay(conv_w, conv_dim),
            lay(n_groups, pg, pg),
            lay(1, pool_dim),
            lay(1, q_rank),
            lay(1, kv_rank),
            lay(q_rank, n_heads * hw),
            lay(kv_rank, n_heads * dn),
            lay(1, hw),
            lay(1, dn),
            lay(1, ROPE_SPREAD),
            lay(1, ROPE_SPREAD),
        ],
        out_specs=[
            full(rows, mix_dim),
            full(rows, conv_dim),
            full(rows, kv_rank),
            full(n_heads, rows, kv_rank),
            full(n_heads, rows, ROPE_SPREAD),
            full(n_heads, rows, ROPE_SPREAD),
            full(rows, n_heads),
        ],
        out_shape=[
            jax.ShapeDtypeStruct((rows, mix_dim), BF16),
            jax.ShapeDtypeStruct((rows, conv_dim), F32),
            jax.ShapeDtypeStruct((rows, kv_rank), F32),
            jax.ShapeDtypeStruct((n_heads, rows, kv_rank), F32),
            jax.ShapeDtypeStruct((n_heads, rows, ROPE_SPREAD), F32),
            jax.ShapeDtypeStruct((n_heads, rows, ROPE_SPREAD), F32),
            jax.ShapeDtypeStruct((rows, n_heads), F32),
        ],
        compiler_params=_params(("arbitrary",)),
        name="mix_sample",
    )(z, cst, pst, cc, ss, lw["w_conv"], lw["w_pool"], lw["s_pool"], lw["g_cq"], lw["g_ckv"],
      lw["w_uq"], lw["w_uk"], lw["g_q"], lw["g_kn"], lw["g_kr"], lw["g_krb"])


def _attn_prompt_kernel(q_ref, k_ref, v_ref, wuv_ref, o_ref, m_s, l_s, acc_s, *, tq):
    i = pl.program_id(2)
    q = q_ref[...]
    m_s[...] = jnp.full(m_s.shape, -jnp.inf, F32)
    l_s[...] = jnp.zeros(l_s.shape, F32)
    acc_s[...] = jnp.zeros(acc_s.shape, F32)

    def step(j, masked):
        start = pl.multiple_of(j * tq, tq)
        s = _nt_dot(q, k_ref[pl.ds(start, tq), :])
        if masked:
            row = lax.broadcasted_iota(jnp.int32, s.shape, 0)
            col = lax.broadcasted_iota(jnp.int32, s.shape, 1)
            s = jnp.where(col <= row, s, -jnp.inf)
        m_prev = m_s[...]
        m_new = jnp.maximum(m_prev, jnp.max(s, axis=-1, keepdims=True))
        p = jnp.exp(s - m_new)
        corr = jnp.exp(m_prev - m_new)
        l_s[...] = l_s[...] * corr + jnp.sum(p, axis=-1, keepdims=True)
        acc_s[...] = acc_s[...] * corr + jnp.dot(p.astype(BF16), v_ref[pl.ds(start, tq), :],
                                                 preferred_element_type=F32)
        m_s[...] = m_new

    def body(j, carry):
        step(j, False)
        return carry

    lax.fori_loop(0, i, body, 0)
    step(i, True)
    lat = (acc_s[...] / l_s[...]).astype(BF16)
    o_ref[...] = jnp.dot(lat, wuv_ref[...], preferred_element_type=F32).astype(o_ref.dtype)


def _attn_prompt(q, k, vb, w_uv, layer, tq):
    n_batch, n_heads, n_tok, hw = q.shape
    kv_rank = vb.shape[-1]
    dv = w_uv.shape[-1]
    tiles = n_tok // tq
    return pl.pallas_call(
        functools.partial(_attn_prompt_kernel, tq=tq),
        grid=(n_batch, n_heads, tiles),
        in_specs=[
            pl.BlockSpec((None, None, tq, hw), lambda b, h, i: (b, h, i, 0)),
            pl.BlockSpec((None, None, n_tok, hw), lambda b, h, i: (b, h, 0, 0)),
            pl.BlockSpec((None, n_tok, kv_rank), lambda b, h, i: (b, 0, 0)),
            pl.BlockSpec((None, None, kv_rank, dv), lambda b, h, i: (layer, h, 0, 0)),
        ],
        out_specs=pl.BlockSpec((tq, dv), lambda b, h, i: (b * tiles + i, h)),
        out_shape=jax.ShapeDtypeStruct((n_batch * n_tok, n_heads * dv), BF16),
        scratch_shapes=[
            pltpu.VMEM((tq, 1), F32),
            pltpu.VMEM((tq, 1), F32),
            pltpu.VMEM((tq, kv_rank), F32),
        ],
        compiler_params=_params(("arbitrary", "arbitrary", "arbitrary")),
        name="attn_prompt",
    )(q, k, vb, w_uv)


def _attn_sample_kernel(pt_ref, wukt_ref, qt_ref, qa_ref, qb_ref, sself_ref, cnew_ref, cc_ref, ss_ref,
                        cache_ckv, cache_kr, lat_ref,
                        wext, cbuf, rbuf, sems, m_s, l_s, acc_s, *, dims):
    (layer, n_req, n_chunks, pages_per_chunk, page, n_heads, dn, dqk) = dims
    n_rows = n_heads * dn
    tok = pages_per_chunk * page
    n_total = n_req * n_chunks
    pad_rows = qt_ref.shape[1]

    def copies(g, slot):
        b = g // n_chunks
        c = g % n_chunks
        out = []
        for p in range(pages_per_chunk):
            pid = pt_ref[b, c * pages_per_chunk + p]
            out.append(pltpu.make_async_copy(cache_ckv.at[layer, pid],
                                             cbuf.at[slot, pl.ds(p * page, page)], sems.at[0, slot, p]))
            out.append(pltpu.make_async_copy(cache_kr.at[layer, pid],
                                             rbuf.at[slot, pl.ds(p * page, page)], sems.at[1, slot, p]))
        return out

    wext[0:n_rows, :] = wukt_ref[...]
    for cp in copies(0, 0):
        cp.start()

    def body(g, carry):
        slot = g % 2
        b = g // n_chunks
        c = g % n_chunks

        @pl.when(g + 1 < n_total)
        def _():
            for cp in copies(g + 1, 1 - slot):
                cp.start()

        for cp in copies(g, slot):
            cp.wait()

        @pl.when(c == 0)
        def _():
            wext[n_rows:n_rows + pad_rows, :] = qt_ref[b]
            m_s[...] = sself_ref[b]
            l_s[...] = jnp.ones(l_s.shape, F32)
            acc_s[...] = jnp.broadcast_to(cnew_ref[b], acc_s.shape)

        cb = cbuf[slot].astype(BF16)
        big = _nt_dot(wext[...], cb)
        kt = big[0:n_rows].reshape(n_heads, dn, tok)
        ssq = jnp.sum(kt * kt, axis=1)
        sn = big[n_rows:n_rows + n_heads]
        kr = rbuf[slot]
        rows = pl.ds(pl.multiple_of(c * tok, tok), tok)
        ka = (kr * cc_ref[rows, :]).astype(BF16)
        kb = (kr * ss_ref[rows, :]).astype(BF16)
        k2 = (kr * kr).astype(BF16)
        sa = _nt_dot(qa_ref[b], ka)[0:n_heads]
        sb = _nt_dot(qb_ref[b], kb)[0:n_heads]
        krsq = _nt_dot(jnp.ones((pad_rows, kr.shape[1]), BF16), k2)[0:1]
        s = (sn + sa + sb) * lax.rsqrt((ssq + krsq) / dqk + EPS)

        m_prev = m_s[...]
        m_new = jnp.maximum(m_prev, jnp.max(s, axis=-1, keepdims=True))
        p = jnp.exp(s - m_new)
        corr = jnp.exp(m_prev - m_new)
        l_s[...] = l_s[...] * corr + jnp.sum(p, axis=-1, keepdims=True)
        p16 = jnp.concatenate([p, jnp.zeros((pad_rows - n_heads, tok), F32)], axis=0).astype(BF16)
        pv = jnp.dot(p16, cb, preferred_element_type=F32)[0:n_heads]
        acc_s[...] = acc_s[...] * corr + pv
        m_s[...] = m_new

        @pl.when(c == n_chunks - 1)
        def _():
            lat_ref[b] = acc_s[...] / l_s[...]

        return carry

    lax.fori_loop(0, n_total, body, 0)


def _attn_sample(page_table, wukt, qt, qa, qb, sself, cnew, cc2, ss2, cache_ckv, cache_kr,
                 layer, pages_per_chunk, n_heads, dn, dqk):
    n_req, n_pages = page_table.shape
    page = cache_ckv.shape[2]
    kv_rank = cache_ckv.shape[3]
    rope = cache_kr.shape[3]
    tok = pages_per_chunk * page
    n_chunks = n_pages // pages_per_chunk
    pad_rows = qt.shape[1]
    dims = (layer, n_req, n_chunks, pages_per_chunk, page, n_heads, dn, dqk)
    full = lambda *shape: _resident(shape, lambda i, pt: (0,) * len(shape))
    grid_spec = pltpu.PrefetchScalarGridSpec(
        num_scalar_prefetch=1,
        grid=(1,),
        in_specs=[
            full(n_heads * dn, kv_rank),
            full(n_req, pad_rows, kv_rank),
            full(n_req, pad_rows, rope),
            full(n_req, pad_rows, rope),
            full(n_req, n_heads, 1),
            full(n_req, 1, kv_rank),
            full(n_pages * page, rope),
            full(n_pages * page, rope),
            pl.BlockSpec(memory_space=pl.ANY),
            pl.BlockSpec(memory_space=pl.ANY),
        ],
        out_specs=pl.BlockSpec((n_req, n_heads, kv_rank), lambda i, pt: (0, 0, 0)),
        scratch_shapes=[
            pltpu.VMEM((n_heads * dn + pad_rows, kv_rank), BF16),
            pltpu.VMEM((2, tok, kv_rank), F32),
            pltpu.VMEM((2, tok, rope), F32),
            pltpu.SemaphoreType.DMA((2, 2, pages_per_chunk)),
            pltpu.VMEM((n_heads, 1), F32),
            pltpu.VMEM((n_heads, 1), F32),
            pltpu.VMEM((n_heads, kv_rank), F32),
        ],
    )
    return pl.pallas_call(
        functools.partial(_attn_sample_kernel, dims=dims),
        grid_spec=grid_spec,
        out_shape=jax.ShapeDtypeStruct((n_req, n_heads, kv_rank), F32),
        compiler_params=_params(("arbitrary",)),
        name="attn_sample",
    )(page_table, wukt, qt, qa, qb, sself, cnew, cc2, ss2, cache_ckv, cache_kr)


def _outproj_kernel(mix_ref, ya_ref, x_ref, g1_ref, sh2_ref, sc2_ref, gffn_ref, wo_ref, x1_ref, h2_ref,
                    *, mix_dim):
    y = jnp.dot(mix_ref[...], wo_ref[0:mix_dim, :], preferred_element_type=F32)
    y = y + jnp.dot(ya_ref[...], wo_ref[mix_dim:, :], preferred_element_type=F32)
    x1 = x_ref[...] + g1_ref[...] * y
    x1_ref[...] = x1
    h2_ref[...] = (_rms(x1, gffn_ref[...]) * (1.0 + sc2_ref[...]) + sh2_ref[...]).astype(BF16)


def _outproj_sample_kernel(mix_ref, lat_ref, wuv_ref, x_ref, g1_ref, sh2_ref, sc2_ref, gffn_ref, wo_ref,
                           x1_ref, h2_ref, *, mix_dim, n_heads, dv):
    y = jnp.dot(mix_ref[...], wo_ref[0:mix_dim, :], preferred_element_type=F32)
    for h in range(n_heads):
        ya = jnp.dot(lat_ref[h].astype(BF16), wuv_ref[h], preferred_element_type=F32).astype(BF16)
        y = y + jnp.dot(ya, wo_ref[mix_dim + h * dv:mix_dim + (h + 1) * dv, :], preferred_element_type=F32)
    x1 = x_ref[...] + g1_ref[...] * y
    x1_ref[...] = x1
    h2_ref[...] = (_rms(x1, gffn_ref[...]) * (1.0 + sc2_ref[...]) + sh2_ref[...]).astype(BF16)


def _outproj(mix, ya, x, mod, g_ffn, w_o, layer, tm, rows_per_batch):
    m, d_model = x.shape
    mix_dim = mix.shape[-1]
    att_dim = ya.shape[-1]
    return pl.pallas_call(
        functools.partial(_outproj_kernel, mix_dim=mix_dim),
        grid=(m // tm,),
        in_specs=[
            pl.BlockSpec((tm, mix_dim), lambda i: (i, 0)),
            pl.BlockSpec((tm, att_dim), lambda i: (i, 0)),
            pl.BlockSpec((tm, d_model), lambda i: (i, 0)),
            _mod_spec(mod, layer, 2, tm, rows_per_batch),
            _mod_spec(mod, layer, 3, tm, rows_per_batch),
            _mod_spec(mod, layer, 4, tm, rows_per_batch),
            pl.BlockSpec((None, 1, d_model), lambda i: (layer, 0, 0)),
            _resident((None, mix_dim + att_dim, d_model), lambda i: (layer, 0, 0)),
        ],
        out_specs=[
            pl.BlockSpec((tm, d_model), lambda i: (i, 0)),
            pl.BlockSpec((tm, d_model), lambda i: (i, 0)),
        ],
        out_shape=[
            jax.ShapeDtypeStruct((m, d_model), F32),
            jax.ShapeDtypeStruct((m, d_model), BF16),
        ],
        compiler_params=_params(("arbitrary",)),
        name="outproj",
    )(mix, ya, x, mod, mod, mod, g_ffn, w_o)


def _outproj_sample(mix, lat_t, w_uv, x, mod, g_ffn, w_o, layer):
    m, d_model = x.shape
    mix_dim = mix.shape[-1]
    n_heads, _, kv_rank = lat_t.shape
    dv = w_uv.shape[-1]
    return pl.pallas_call(
        functools.partial(_outproj_sample_kernel, mix_dim=mix_dim, n_heads=n_heads, dv=dv),
        grid=(1,),
        in_specs=[
            pl.BlockSpec((m, mix_dim), lambda i: (0, 0)),
            pl.BlockSpec((n_heads, m, kv_rank), lambda i: (0, 0, 0)),
            pl.BlockSpec((None, n_heads, kv_rank, dv), lambda i: (layer, 0, 0, 0)),
            pl.BlockSpec((m, d_model), lambda i: (0, 0)),
            _mod_spec(mod, layer, 2, m, 1),
            _mod_spec(mod, layer, 3, m, 1),
            _mod_spec(mod, layer, 4, m, 1),
            pl.BlockSpec((None, 1, d_model), lambda i: (layer, 0, 0)),
            _resident((None, mix_dim + n_heads * dv, d_model), lambda i: (layer, 0, 0)),
        ],
        out_specs=[
            pl.BlockSpec((m, d_model), lambda i: (0, 0)),
            pl.BlockSpec((m, d_model), lambda i: (0, 0)),
        ],
        out_shape=[
            jax.ShapeDtypeStruct((m, d_model), F32),
            jax.ShapeDtypeStruct((m, d_model), BF16),
        ],
        compiler_params=_params(("arbitrary",)),
        name="outproj_sample",
    )(mix, lat_t, w_uv, x, mod, mod, mod, g_ffn, w_o)


def _ffn_kernel(h_ref, x1_ref, g2_ref, w1_ref, w3_ref, w2_ref, o_ref, acc_s):
    k = pl.program_id(1)

    @pl.when(k == 0)
    def _():
        acc_s[...] = jnp.zeros(acc_s.shape, F32)

    h = h_ref[...]
    a = jnp.dot(h, w1_ref[...], preferred_element_type=F32)
    b = jnp.dot(h, w3_ref[...], preferred_element_type=F32)
    f = (a * jax.nn.sigmoid(a) * b).astype(BF16)
    acc_s[...] += jnp.dot(f, w2_ref[...], preferred_element_type=F32)

    @pl.when(k == pl.num_programs(1) - 1)
    def _():
        o_ref[...] = x1_ref[...] + g2_ref[...] * acc_s[...]


def _ffn(h2, x1, mod, w1, w3, w2, layer, tm, fc, rows_per_batch):
    m, d_model = x1.shape
    d_ff = w1.shape[-1]
    mod_spec = _mod_spec(mod, layer, 5, tm, rows_per_batch)
    mod_map = mod_spec.index_map
    return pl.pallas_call(
        _ffn_kernel,
        grid=(m // tm, d_ff // fc),
        in_specs=[
            pl.BlockSpec((tm, d_model), lambda i, k: (i, 0)),
            pl.BlockSpec((tm, d_model), lambda i, k: (i, 0)),
            pl.BlockSpec(mod_spec.block_shape, lambda i, k: mod_map(i)),
            pl.BlockSpec((None, d_model, fc), lambda i, k: (layer, 0, k)),
            pl.BlockSpec((None, d_model, fc), lambda i, k: (layer, 0, k)),
            pl.BlockSpec((None, fc, d_model), lambda i, k: (layer, k, 0)),
        ],
        out_specs=pl.BlockSpec((tm, d_model), lambda i, k: (i, 0)),
        out_shape=jax.ShapeDtypeStruct((m, d_model), F32),
        scratch_shapes=[pltpu.VMEM((tm, d_model), F32)],
        compiler_params=_params(("arbitrary", "arbitrary")),
        name="ffn",
    )(h2, x1, mod, w1, w3, w2)


def _spread_last(x, half):
    pad = [(0, 0)] * (x.ndim - 1) + [(0, ROPE_HALF_OFF - half)]
    return jnp.concatenate([jnp.pad(x[..., :half], pad), jnp.pad(x[..., half:], pad)], axis=-1)


def _compact_last(x, half):
    return jnp.concatenate([x[..., :half], x[..., ROPE_HALF_OFF:ROPE_HALF_OFF + half]], axis=-1)


def _rope_tables(pos, half):
    inv = ROPE_THETA ** (-jnp.arange(half, dtype=F32) / half)
    ang = pos.astype(F32)[:, None] * inv[None, :]
    return jnp.cos(ang), jnp.sin(ang)


def kernel(x_prompt, x_sample, cache_ckv, cache_kr, state_conv, state_pool, page_table,
           c_prompt, c_sample, g_mix, w_ada, b_ada, w_in, w_conv, w_pool, s_pool,
           g_cq, g_ckv, w_uq, w_uk, w_uv, g_q, g_k, w_o, g_ffn, w1, w3, w2):
    n_batch, n_tok, d_model = x_prompt.shape
    n_req, dec_seq, _ = x_sample.shape
    assert dec_seq == 1, "the sample group carries one new token per request"
    n_layers = w_in.shape[0]
    conv_w, conv_dim = w_conv.shape[1:]
    pool_dim = s_pool.shape[1]
    pool_ctx = state_pool.shape[2]
    assert pool_ctx == max(POOL_WINDOWS) - 1 and w_pool.shape[1] == len(POOL_WINDOWS)
    q_rank = g_cq.shape[1]
    kv_rank = g_ckv.shape[1]
    n_heads, dqk = w_uq.shape[2:]
    dn = w_uk.shape[3]
    rope = dqk - dn
    half = rope // 2
    dv = w_uv.shape[3]
    page = cache_ckv.shape[2]
    n_pages = page_table.shape[1]
    past = n_pages * page
    base = w_in.shape[2] - rope
    assert base == 3 * conv_dim + pool_dim + q_rank + kv_rank and half <= ROPE_HALF_OFF
    assert dn % LANES == 0 and n_tok >= 2 * SUBLANES
    cfg = (conv_dim, pool_dim, q_rank, kv_rank, n_heads, dn, dqk, conv_w, pool_ctx)

    w_in_p = jnp.concatenate([w_in[..., :base], _spread_last(w_in[..., base:], half)], axis=-1).astype(BF16)
    w_uq_p = jnp.concatenate([w_uq[..., :dn], _spread_last(w_uq[..., dn:], half)], axis=-1)
    w_uq_p = w_uq_p.reshape(n_layers, q_rank, n_heads * (dn + ROPE_SPREAD)).astype(BF16)
    g_q_p = jnp.concatenate([g_q[:, :dn], _spread_last(g_q[:, dn:], half)], axis=-1)[:, None, :]
    g_kr = _spread_last(g_k[:, dn:], half)[:, None, :]
    g_krb = _spread_last(jnp.concatenate([g_k[:, dn:dn + half], -g_k[:, dn + half:]], axis=-1), half)[:, None, :]
    w_uk_b = w_uk.reshape(n_layers, kv_rank, n_heads * dn).astype(BF16)
    lw = {
        "w_conv": w_conv, "w_pool": w_pool.astype(BF16), "s_pool": s_pool[:, None, :],
        "g_cq": g_cq[:, None, :], "g_ckv": g_ckv[:, None, :], "w_uq": w_uq_p, "w_uk": w_uk_b,
        "g_q": g_q_p, "g_kn": g_k[:, None, :dn], "g_kr": g_kr, "g_krb": g_krb,
    }
    w_ukt = jnp.swapaxes(w_uk_b, 1, 2)
    w_uv_b = jnp.transpose(w_uv, (0, 2, 1, 3)).astype(BF16)
    w_o_b = w_o.astype(BF16)
    w1_b, w3_b, w2_b = w1.astype(BF16), w3.astype(BF16), w2.astype(BF16)
    g_mix3, g_ffn3 = g_mix[:, None, :], g_ffn[:, None, :]

    cos_p, sin_p = _rope_tables(jnp.arange(n_tok), half)
    cc_p = _spread_last(jnp.concatenate([cos_p, cos_p], axis=-1), half)
    ss_p = _spread_last(jnp.concatenate([-sin_p, sin_p], axis=-1), half)
    cos_s, sin_s = _rope_tables(past + jnp.arange(dec_seq), half)
    cc_s = _spread_last(jnp.concatenate([cos_s, cos_s], axis=-1), half)
    ss_s = _spread_last(jnp.concatenate([-sin_s, sin_s], axis=-1), half)
    cos_c, sin_c = _rope_tables(jnp.arange(past), half)
    cc_c = jnp.concatenate([cos_c, cos_c], axis=-1)
    ss_c = jnp.concatenate([sin_c, sin_c], axis=-1)

    tm_p = math.gcd(n_tok, 512)
    tc_p = math.gcd(n_tok, 512)
    tq_p = math.gcd(n_tok, 512)
    fc = math.gcd(w1.shape[2], 512)
    pages_per_chunk = math.gcd(n_pages, 4)
    pad_rows = 2 * SUBLANES

    mod = _ada_mod(jnp.concatenate([c_prompt, c_sample], axis=0), w_ada, b_ada, math.gcd(d_model, 1024))
    mod_p = mod[:, :, :n_batch, None, :]
    mod_s = mod[:, :, n_batch:, :]

    st_conv = jnp.swapaxes(state_conv, 1, 2)
    st_pool = jnp.swapaxes(state_pool, 1, 2)

    xp = x_prompt.reshape(n_batch * n_tok, d_model)
    xs = x_sample.reshape(n_req, d_model)
    outs = {k: [] for k in ("ckv_p", "kr_p", "conv_p", "pool_p", "ckv_s", "kr_s", "conv_s", "pool_s")}
    for l in range(n_layers):
        z = _inproj(xp, mod_p, g_mix3, w_in_p, l, tm_p, n_tok)
        mix, q, k, ckvn, vb, kr_sp, cst, pst = _mix_prompt(z, cc_p, ss_p, lw, l, n_batch, n_tok, tc_p, cfg)
        ya = _attn_prompt(q, k, vb, w_uv_b, l, tq_p)
        x1, h2 = _outproj(mix, ya, xp, mod_p, g_ffn3, w_o_b, l, tm_p, n_tok)
        xp = _ffn(h2, x1, mod_p, w1_b, w3_b, w2_b, l, tm_p, fc, n_tok)
        outs["ckv_p"].append(ckvn)
        outs["kr_p"].append(_compact_last(kr_sp, half))
        outs["conv_p"].append(cst)
        outs["pool_p"].append(pst)

        z = _inproj(xs, mod_s, g_mix3, w_in_p, l, n_req, 1)
        mix, u_new, ckv_new, qt, qa, qb, sself = _mix_sample(z, st_conv, st_pool, cc_s, ss_s, lw, l, cfg, past)
        pad3 = lambda a: jnp.pad(jnp.swapaxes(a, 0, 1), ((0, 0), (0, pad_rows - n_heads), (0, 0))).astype(BF16)
        lat = _attn_sample(page_table, w_ukt[l], pad3(qt), pad3(_compact_last(qa, half)),
                           pad3(_compact_last(qb, half)), sself[:, :, None], ckv_new[:, None, :],
                           cc_c, ss_c, cache_ckv, cache_kr, l, pages_per_chunk, n_heads, dn, dqk)
        x1, h2 = _outproj_sample(mix, jnp.swapaxes(lat, 0, 1), w_uv_b, xs, mod_s, g_ffn3, w_o_b, l)
        xs = _ffn(h2, x1, mod_s, w1_b, w3_b, w2_b, l, n_req, fc, 1)
        outs["ckv_s"].append(ckv_new[:, None, :])
        outs["kr_s"].append(_compact_last(z[:, base:], half)[:, None, :])
        outs["conv_s"].append(jnp.concatenate([state_conv[l][:, 1:], u_new[:, None, :]], axis=1))
        outs["pool_s"].append(jnp.concatenate([state_pool[l][:, 1:], z[:, None, 3 * conv_dim:3 * conv_dim + pool_dim]], axis=1))

    st = {k: jnp.stack(v) for k, v in outs.items()}
    return (xp.reshape(n_batch, n_tok, d_model), xs.reshape(n_req, dec_seq, d_model),
            st["ckv_p"], st["kr_p"], st["conv_p"], st["pool_p"],
            st["ckv_s"], st["kr_s"], st["conv_s"], st["pool_s"])
```

```python
import functools
import math

import jax
import jax.numpy as jnp
from jax import lax
from jax.experimental import pallas as pl
from jax.experimental.pallas import tpu as pltpu

F32 = jnp.float32
BF16 = jnp.bfloat16

EPS = 1e-6
ROPE_THETA = 10000.0
POOL_WINDOWS = (2, 4, 8, 16)

LANES = 128
SUBLANES = 8
VMEM_BYTES_V7X = 64 * 1024 * 1024
VMEM_LIMIT = VMEM_BYTES_V7X - 8 * 1024 * 1024

ROPE_SPREAD = LANES
ROPE_HALF_OFF = LANES // 2
HEAD_SKEW = 3
FFN_SPLIT = 2
DMA_LOOKAHEAD = 2
DMA_SLOTS = DMA_LOOKAHEAD + 1


def _params(semantics):
    return pltpu.CompilerParams(dimension_semantics=semantics, vmem_limit_bytes=VMEM_LIMIT)


def _resident(block_shape, index_map):
    return pl.BlockSpec(block_shape, index_map, pipeline_mode=pl.Buffered(1))


def _rms(x, g):
    return x * lax.rsqrt(jnp.mean(x * x, axis=-1, keepdims=True) + EPS) * g


def _nt_dot(a, b):
    return lax.dot_general(a, b, (((1,), (1,)), ((), ())), preferred_element_type=F32)


def _rope_spread(x, cc, ss):
    return x * cc + pltpu.roll(x, ROPE_HALF_OFF, 1) * ss


def _ada_kernel(c_ref, w_ref, b_ref, o_ref):
    c = c_ref[...]
    a = (c * jax.nn.sigmoid(c)).astype(BF16)
    o_ref[...] = jnp.dot(a, w_ref[...].astype(BF16), preferred_element_type=F32) + b_ref[...]


def _ada_mod(c_all, w_ada, b_ada, tn):
    n_layers, d_model, six_d = w_ada.shape
    rows = c_all.shape[0]
    n_per = d_model // tn
    return pl.pallas_call(
        _ada_kernel,
        grid=(n_layers, 6, n_per),
        in_specs=[
            pl.BlockSpec((rows, d_model), lambda l, k, n: (0, 0)),
            pl.BlockSpec((None, d_model, tn), lambda l, k, n: (l, 0, k * n_per + n)),
            pl.BlockSpec((None, 1, tn), lambda l, k, n: (l, 0, k * n_per + n)),
        ],
        out_specs=pl.BlockSpec((None, None, rows, tn), lambda l, k, n: (l, k, 0, n)),
        out_shape=jax.ShapeDtypeStruct((n_layers, 6, rows, d_model), F32),
        compiler_params=_params(("arbitrary", "arbitrary", "arbitrary")),
        name="ada_mod",
    )(c_all, w_ada, b_ada.reshape(n_layers, 1, six_d))


def _mod_spec(mod, layer, chunk, tm, rows_per_batch):
    if mod.ndim == 5:
        tiles_per_batch = rows_per_batch // tm
        return pl.BlockSpec((None, None, None, 1, mod.shape[-1]),
                            lambda i: (layer, chunk, i // tiles_per_batch, 0, 0))
    return pl.BlockSpec((None, None, tm, mod.shape[-1]), lambda i: (layer, chunk, i, 0))


def _inproj_kernel(x_ref, sh_ref, sc_ref, g_ref, w_ref, wkr_ref, z_ref):
    h = (_rms(x_ref[...], g_ref[...]) * (1.0 + sc_ref[...]) + sh_ref[...]).astype(BF16)
    n_main = w_ref.shape[-1]
    z_ref[:, 0:n_main] = jnp.dot(h, w_ref[...], preferred_element_type=F32)
    z_ref[:, n_main:] = jnp.dot(h, wkr_ref[...], preferred_element_type=F32)


def _inproj(x, mod, g_mix, w_in_b, w_kr, n_main, layer, tm, rows_per_batch):
    m, d_model = x.shape
    assert n_main % LANES == 0
    n_out = n_main + w_kr.shape[-1]
    return pl.pallas_call(
        _inproj_kernel,
        grid=(m // tm,),
        in_specs=[
            pl.BlockSpec((tm, d_model), lambda i: (i, 0)),
            _mod_spec(mod, layer, 0, tm, rows_per_batch),
            _mod_spec(mod, layer, 1, tm, rows_per_batch),
            pl.BlockSpec((None, 1, d_model), lambda i: (layer, 0, 0)),
            _resident((None, d_model, n_main), lambda i: (layer, 0, 0)),
            _resident((None, d_model, w_kr.shape[-1]), lambda i: (layer, 0, 0)),
        ],
        out_specs=pl.BlockSpec((tm, n_out), lambda i: (i, 0)),
        out_shape=jax.ShapeDtypeStruct((m, n_out), F32),
        compiler_params=_params(("arbitrary",)),
        name="inproj",
    )(x, mod, mod, g_mix, w_in_b, w_kr)


def _queries(cq, gcq, wuq, gq, cc, ss, n_heads, dn, dqk, scale):
    cqn = _rms(cq, gcq).astype(BF16)
    qall = jnp.dot(cqn, wuq, preferred_element_type=F32)
    hw = dn + ROPE_SPREAD
    out = []
    for h in range(n_heads):
        qn = qall[:, h * hw:h * hw + dn]
        qr = qall[:, h * hw + dn:(h + 1) * hw]
        ssq = jnp.sum(qn * qn, axis=-1, keepdims=True) + jnp.sum(qr * qr, axis=-1, keepdims=True)
        r = lax.rsqrt(ssq / dqk + EPS) * scale
        qn = qn * r * gq[:, :dn]
        qr = _rope_spread(qr * r * gq[:, dn:], cc, ss)
        out.append((qn, qr))
    return out


def _keys(ckvn_b, kr, wuk, gkn, gkr, cc, ss, n_heads, dn, dqk):
    kn_all = jnp.dot(ckvn_b, wuk, preferred_element_type=F32)
    krsq = jnp.sum(kr * kr, axis=-1, keepdims=True)
    krot = _rope_spread(kr * gkr, cc, ss)
    out = []
    for h in range(n_heads):
        kn = kn_all[:, h * dn:(h + 1) * dn]
        r = lax.rsqrt((jnp.sum(kn * kn, axis=-1, keepdims=True) + krsq) / dqk + EPS)
        out.append((kn * r * gkn, krot * r))
    return out


def _mix_prompt_kernel(z_ref, cc_ref, ss_ref, wconv_ref, wpool_ref, spool_ref, gcq_ref, gckv_ref,
                       wuq_ref, wuk_ref, gq_ref, gkn_ref, gkr_ref,
                       mix_ref, q_ref, k_ref, ckv_ref, vb_ref, kr_ref, cst_ref, pst_ref,
                       ubuf, vbuf, *, dims):
    (tc, conv_dim, pool_dim, q_rank, kv_rank, n_heads, dn, dqk, scale, conv_w, pool_ctx) = dims
    j = pl.program_id(1)
    halo_u = SUBLANES
    halo_v = 2 * SUBLANES

    @pl.when(j == 0)
    def _():
        ubuf[0:halo_u, :] = jnp.zeros((halo_u, conv_dim), F32)
        vbuf[0:halo_v, :] = jnp.zeros((halo_v, pool_dim), F32)

    o = 0
    hc = z_ref[:, o:o + conv_dim]; o += conv_dim
    bc = z_ref[:, o:o + conv_dim]; o += conv_dim
    cg = z_ref[:, o:o + conv_dim]; o += conv_dim
    vp = z_ref[:, o:o + pool_dim]; o += pool_dim
    cq = z_ref[:, o:o + q_rank]; o += q_rank
    ckv = z_ref[:, o:o + kv_rank]; o += kv_rank
    kr = z_ref[:, o:o + ROPE_SPREAD]

    u = cg * hc
    ubuf[halo_u:halo_u + tc, :] = u
    acc = wconv_ref[conv_w - 1:conv_w, :] * u
    for t in range(1, conv_w):
        acc = acc + wconv_ref[conv_w - 1 - t:conv_w - t, :] * ubuf[pl.ds(halo_u - t, tc), :]
    mix_ref[:, 0:conv_dim] = (bc * acc).astype(BF16)
    cst_ref[...] = ubuf[pl.ds(halo_u + tc - (conv_w - 1), conv_w - 1), :]
    ubuf[0:halo_u, :] = ubuf[tc:tc + halo_u, :]

    vbuf[halo_v:halo_v + tc, :] = vp
    pg = pool_dim // len(POOL_WINDOWS)
    pos = j * tc + lax.broadcasted_iota(jnp.int32, (tc, pg), 0)
    for g, w in enumerate(POOL_WINDOWS):
        sl = slice(g * pg, (g + 1) * pg)
        v = vp[:, sl]
        s = v
        for t in range(1, w):
            s = s + vbuf[pl.ds(halo_v - t, tc), sl]
        cnt = jnp.minimum(pos + 1, w).astype(F32)
        d = s / cnt - v
        y = jnp.dot(d.astype(BF16), wpool_ref[g], preferred_element_type=F32) * spool_ref[:, sl]
        mix_ref[:, conv_dim + g * pg:conv_dim + (g + 1) * pg] = y.astype(BF16)
    pst_ref[...] = vbuf[pl.ds(halo_v + tc - pool_ctx, pool_ctx), :]
    vbuf[0:halo_v, :] = vbuf[tc:tc + halo_v, :]

    cc = cc_ref[...]
    ss = ss_ref[...]
    qs = _queries(cq, gcq_ref[...], wuq_ref[...], gq_ref[...], cc, ss, n_heads, dn, dqk, scale)
    for h, (qn, qr) in enumerate(qs):
        q_ref[h, :, 0:dn] = qn.astype(BF16)
        q_ref[h, :, dn:dn + ROPE_SPREAD] = qr.astype(BF16)
    ckvn = _rms(ckv, gckv_ref[...])
    ckv_ref[...] = ckvn
    ckvn_b = ckvn.astype(BF16)
    vb_ref[...] = ckvn_b
    kr_ref[...] = kr
    ks = _keys(ckvn_b, kr, wuk_ref[...], gkn_ref[...], gkr_ref[...], cc, ss, n_heads, dn, dqk)
    for h, (kn, krot) in enumerate(ks):
        k_ref[h, :, 0:dn] = kn.astype(BF16)
        k_ref[h, :, dn:dn + ROPE_SPREAD] = krot.astype(BF16)


def _mix_prompt(z, cc, ss, lw, layer, n_batch, n_tok, tc, cfg):
    (conv_dim, pool_dim, q_rank, kv_rank, n_heads, dn, dqk, conv_w, pool_ctx) = cfg
    n_in = z.shape[-1]
    hw = dn + ROPE_SPREAD
    tiles = n_tok // tc
    dims = (tc, conv_dim, pool_dim, q_rank, kv_rank, n_heads, dn, dqk, dqk ** -0.5 * math.log2(math.e),
            conv_w, pool_ctx)
    lay = lambda *shape: pl.BlockSpec((None,) + shape, lambda b, j: (layer,) + (0,) * len(shape))
    n_groups = len(POOL_WINDOWS)
    pg = pool_dim // n_groups
    mix_dim = conv_dim + pool_dim
    return pl.pallas_call(
        functools.partial(_mix_prompt_kernel, dims=dims),
        grid=(n_batch, tiles),
        in_specs=[
            pl.BlockSpec((tc, n_in), lambda b, j: (b * tiles + j, 0)),
            pl.BlockSpec((tc, ROPE_SPREAD), lambda b, j: (j, 0)),
            pl.BlockSpec((tc, ROPE_SPREAD), lambda b, j: (j, 0)),
            lay(conv_w, conv_dim),
            lay(n_groups, pg, pg),
            lay(1, pool_dim),
            lay(1, q_rank),
            lay(1, kv_rank),
            lay(q_rank, n_heads * hw),
            lay(kv_rank, n_heads * dn),
            lay(1, hw),
            lay(1, dn),
            lay(1, ROPE_SPREAD),
        ],
        out_specs=[
            pl.BlockSpec((tc, mix_dim), lambda b, j: (b * tiles + j, 0)),
            pl.BlockSpec((None, n_heads, tc, hw), lambda b, j: (b, 0, j, 0)),
            pl.BlockSpec((None, n_heads, tc, hw), lambda b, j: (b, 0, j, 0)),
            pl.BlockSpec((None, tc, kv_rank), lambda b, j: (b, j, 0)),
            pl.BlockSpec((None, tc, kv_rank), lambda b, j: (b, j, 0)),
            pl.BlockSpec((None, tc, ROPE_SPREAD), lambda b, j: (b, j, 0)),
            pl.BlockSpec((None, conv_w - 1, conv_dim), lambda b, j: (b, 0, 0)),
            pl.BlockSpec((None, pool_ctx, pool_dim), lambda b, j: (b, 0, 0)),
        ],
        out_shape=[
            jax.ShapeDtypeStruct((n_batch * n_tok, mix_dim), BF16),
            jax.ShapeDtypeStruct((n_batch, n_heads, n_tok, hw), BF16),
            jax.ShapeDtypeStruct((n_batch, n_heads, n_tok, hw), BF16),
            jax.ShapeDtypeStruct((n_batch, n_tok, kv_rank), F32),
            jax.ShapeDtypeStruct((n_batch, n_tok, kv_rank), BF16),
            jax.ShapeDtypeStruct((n_batch, n_tok, ROPE_SPREAD), F32),
            jax.ShapeDtypeStruct((n_batch, conv_w - 1, conv_dim), F32),
            jax.ShapeDtypeStruct((n_batch, pool_ctx, pool_dim), F32),
        ],
        scratch_shapes=[
            pltpu.VMEM((SUBLANES + tc, conv_dim), F32),
            pltpu.VMEM((2 * SUBLANES + tc, pool_dim), F32),
        ],
        compiler_params=_params(("arbitrary", "arbitrary")),
        name="mix_prompt",
    )(z, cc, ss, lw["w_conv"], lw["w_pool"], lw["s_pool"], lw["g_cq"], lw["g_ckv"],
      lw["w_uq"], lw["w_uk"], lw["g_q"], lw["g_kn"], lw["g_kr"])


def _mix_sample_kernel(z_ref, cst_ref, pst_ref, cc_ref, ss_ref, wconv_ref, wpool_ref, spool_ref,
                       gcq_ref, gckv_ref, wuq_ref, wuk_ref, gq_ref, gkn_ref, gkr_ref, gkrb_ref,
                       mix_ref, u_ref, ckv_ref, qt_ref, qa_ref, qb_ref, sself_ref, *, dims):
    (conv_dim, pool_dim, q_rank, kv_rank, n_heads, dn, dqk, scale, conv_w, pool_ctx, past) = dims
    o = 0
    hc = z_ref[:, o:o + conv_dim]; o += conv_dim
    bc = z_ref[:, o:o + conv_dim]; o += conv_dim
    cg = z_ref[:, o:o + conv_dim]; o += conv_dim
    vp = z_ref[:, o:o + pool_dim]; o += pool_dim
    cq = z_ref[:, o:o + q_rank]; o += q_rank
    ckv = z_ref[:, o:o + kv_rank]; o += kv_rank
    kr = z_ref[:, o:o + ROPE_SPREAD]

    u = cg * hc
    u_ref[...] = u
    acc = wconv_ref[conv_w - 1:conv_w, :] * u
    for t in range(1, conv_w):
        acc = acc + wconv_ref[conv_w - 1 - t:conv_w - t, :] * cst_ref[conv_w - 1 - t]
    mix_ref[:, 0:conv_dim] = (bc * acc).astype(BF16)

    pg = pool_dim // len(POOL_WINDOWS)
    for g, w in enumerate(POOL_WINDOWS):
        sl = slice(g * pg, (g + 1) * pg)
        v = vp[:, sl]
        s = v
        for t in range(1, w):
            s = s + pst_ref[pool_ctx - t, :, sl]
        d = s / float(min(past + 1, w)) - v
        y = jnp.dot(d.astype(BF16), wpool_ref[g], preferred_element_type=F32) * spool_ref[:, sl]
        mix_ref[:, conv_dim + g * pg:conv_dim + (g + 1) * pg] = y.astype(BF16)

    cc = cc_ref[...]
    ss = ss_ref[...]
    qs = _queries(cq, gcq_ref[...], wuq_ref[...], gq_ref[...], cc, ss, n_heads, dn, dqk, scale)
    ckvn = _rms(ckv, gckv_ref[...])
    ckv_ref[...] = ckvn
    wuk = wuk_ref[...]
    gkn = gkn_ref[...]
    ks = _keys(ckvn.astype(BF16), kr, wuk, gkn, gkr_ref[...], cc, ss, n_heads, dn, dqk)
    for h in range(n_heads):
        qn, qr = qs[h]
        kn, krot = ks[h]
        sself_ref[:, h:h + 1] = (jnp.sum(qn * kn, axis=-1, keepdims=True)
                                 + jnp.sum(qr * krot, axis=-1, keepdims=True))
        qt_ref[h] = _nt_dot((qn * gkn).astype(BF16), wuk[:, h * dn:(h + 1) * dn])
        qa_ref[h] = qr * gkr_ref[...]
        qb_ref[h] = pltpu.roll(qr, ROPE_HALF_OFF, 1) * gkrb_ref[...]


def _mix_sample(z, cst, pst, cc, ss, lw, layer, cfg, past):
    (conv_dim, pool_dim, q_rank, kv_rank, n_heads, dn, dqk, conv_w, pool_ctx) = cfg
    rows, n_in = z.shape
    hw = dn + ROPE_SPREAD
    dims = (conv_dim, pool_dim, q_rank, kv_rank, n_heads, dn, dqk, dqk ** -0.5, conv_w, pool_ctx, past)
    lay = lambda *shape: pl.BlockSpec((None,) + shape, lambda i: (layer,) + (0,) * len(shape))
    full = lambda *shape: pl.BlockSpec(shape, lambda i: (0,) * len(shape))
    n_groups = len(POOL_WINDOWS)
    pg = pool_dim // n_groups
    mix_dim = conv_dim + pool_dim
    return pl.pallas_call(
        functools.partial(_mix_sample_kernel, dims=dims),
        grid=(1,),
        in_specs=[
            full(rows, n_in),
            lay(conv_w - 1, rows, conv_dim),
            lay(pool_ctx, rows, pool_dim),
            full(1, ROPE_SPREAD),
            full(1, ROPE_SPREAD),
            lay(conv_w, conv_dim),
            lay(n_groups, pg, pg),
            lay(1, pool_dim),
            lay(1, q_rank),
            lay(1, kv_rank),
            lay(q_rank, n_heads * hw),
            lay(kv_rank, n_heads * dn),
            lay(1, hw),
            lay(1, dn),
            lay(1, ROPE_SPREAD),
            lay(1, ROPE_SPREAD),
        ],
        out_specs=[
            full(rows, mix_dim),
            full(rows, conv_dim),
            full(rows, kv_rank),
            full(n_heads, rows, kv_rank),
            full(n_heads, rows, ROPE_SPREAD),
            full(n_heads, rows, ROPE_SPREAD),
            full(rows, n_heads),
        ],
        out_shape=[
            jax.ShapeDtypeStruct((rows, mix_dim), BF16),
            jax.ShapeDtypeStruct((rows, conv_dim), F32),
            jax.ShapeDtypeStruct((rows, kv_rank), F32),
            jax.ShapeDtypeStruct((n_heads, rows, kv_rank), F32),
            jax.ShapeDtypeStruct((n_heads, rows, ROPE_SPREAD), F32),
            jax.ShapeDtypeStruct((n_heads, rows, ROPE_SPREAD), F32),
            jax.ShapeDtypeStruct((rows, n_heads), F32),
        ],
        compiler_params=_params(("arbitrary",)),
        name="mix_sample",
    )(z, cst, pst, cc, ss, lw["w_conv"], lw["w_pool"], lw["s_pool"], lw["g_cq"], lw["g_ckv"],
      lw["w_uq"], lw["w_uk"], lw["g_q"], lw["g_kn"], lw["g_kr"], lw["g_krb"])


def _attn_prompt_kernel(q_ref, k_ref, v_ref, wuv_ref, o_ref, m_s, l_s, acc_s, *, tq, tk, n_heads, dv):
    i = pl.program_id(1)
    acc_tiles = acc_s.shape[-1] // LANES
    m_s[...] = jnp.full(m_s.shape, -jnp.inf, F32)
    l_s[...] = jnp.zeros(l_s.shape, F32)
    acc_s[...] = jnp.zeros(acc_s.shape, F32)

    def step(start, width, masked):
        v = v_ref[pl.ds(start, width), :]

        def scores(h):
            s = _nt_dot(q_ref[h], k_ref[h, pl.ds(start, width), :])
            if masked:
                row = lax.broadcasted_iota(jnp.int32, s.shape, 0)
                col = lax.broadcasted_iota(jnp.int32, s.shape, 1)
                s = jnp.where(col <= row, s, -jnp.inf)
            return s

        def update(h, s):
            m_prev = m_s[h]
            m_new = jnp.maximum(m_prev, jnp.max(s, axis=-1, keepdims=True))
            corr = jnp.exp2(m_prev - m_new)
            ps = [jnp.exp2(s[:, t * LANES:(t + 1) * LANES] - m_new) for t in range(width // LANES)]
            l_s[h] = l_s[h] * corr + functools.reduce(lambda a, b: a + b, ps)
            p = jnp.concatenate(ps, axis=1).astype(BF16)
            pv = jnp.dot(p, v, preferred_element_type=F32)
            acc_s[h] = acc_s[h] * jnp.concatenate([corr] * acc_tiles, axis=1) + pv
            m_s[h] = m_new

        pending = {}
        for t in range(n_heads + HEAD_SKEW):
            if t < n_heads:
                pending[t] = scores(t)
            if t >= HEAD_SKEW:
                update(t - HEAD_SKEW, pending.pop(t - HEAD_SKEW))

    ratio = tk // tq

    def body(j, carry):
        step(pl.multiple_of(j * tk, tk), tk, False)
        return carry

    lax.fori_loop(0, i // ratio, body, 0)
    for r in range(1, ratio):
        @pl.when(i % ratio >= r)
        def _():
            step(pl.multiple_of((i // ratio) * tk + (r - 1) * tq, tq), tq, False)
    step(pl.multiple_of(i * tq, tq), tq, True)
    for h in range(n_heads):
        inv_l = 1.0 / jnp.sum(l_s[h], axis=-1, keepdims=True)
        lat = (acc_s[h] * inv_l).astype(BF16)
        o_ref[:, h * dv:(h + 1) * dv] = jnp.dot(lat, wuv_ref[h], preferred_element_type=F32).astype(o_ref.dtype)


def _attn_prompt(q, k, vb, w_uv, layer, tq, tk):
    n_batch, n_heads, n_tok, hw = q.shape
    kv_rank = vb.shape[-1]
    dv = w_uv.shape[-1]
    tiles = n_tok // tq
    return pl.pallas_call(
        functools.partial(_attn_prompt_kernel, tq=tq, tk=tk, n_heads=n_heads, dv=dv),
        grid=(n_batch, tiles),
        in_specs=[
            pl.BlockSpec((None, n_heads, tq, hw), lambda b, i: (b, 0, i, 0)),
            _resident((None, n_heads, n_tok, hw), lambda b, i: (b, 0, 0, 0)),
            _resident((None, n_tok, kv_rank), lambda b, i: (b, 0, 0)),
            _resident((None, n_heads, kv_rank, dv), lambda b, i: (layer, 0, 0, 0)),
        ],
        out_specs=pl.BlockSpec((tq, n_heads * dv), lambda b, i: (b * tiles + i, 0)),
        out_shape=jax.ShapeDtypeStruct((n_batch * n_tok, n_heads * dv), BF16),
        scratch_shapes=[
            pltpu.VMEM((n_heads, tq, LANES), F32),
            pltpu.VMEM((n_heads, tq, LANES), F32),
            pltpu.VMEM((n_heads, tq, kv_rank), F32),
        ],
        compiler_params=_params(("arbitrary", "arbitrary")),
        name="attn_prompt",
    )(q, k, vb, w_uv)


def _attn_sample_kernel(pt_ref, trips_ref, wukt_ref, qt_ref, qa_ref, qb_ref, sself_ref, cnew_ref, cc_hbm, ss_hbm,
                        cache_ckv, cache_krt, lat_ref,
                        wext, cbuf, rbuf, sems, m_s, l_s, acc_s, cc_ref, ss_ref, tbl_sems, *, dims):
    (layer, n_req, n_steps, n_sub, pages_per_sub, page, n_heads, dn, dqk) = dims
    n_rows = n_heads * dn
    tok = pages_per_sub * page
    pages_per_step = n_sub * pages_per_sub
    n_total = n_req * n_steps
    pad_rows = wext.shape[0] - n_rows

    def copies(g, slot):
        b = g // n_steps
        c = g % n_steps
        out = []
        for p in range(pages_per_step):
            pid = pt_ref[b, c * pages_per_step + p]
            sub, pp = divmod(p, pages_per_sub)
            out.append(pltpu.make_async_copy(cache_ckv.at[layer, pid],
                                             cbuf.at[slot, sub, pl.ds(pp * page, page)], sems.at[0, slot, sub]))
            out.append(pltpu.make_async_copy(cache_krt.at[layer, pid],
                                             rbuf.at[slot, sub, :, pl.ds(pp * page, page)], sems.at[1, slot, sub]))
        return out

    table_copies = [pltpu.make_async_copy(cc_hbm, cc_ref, tbl_sems.at[0]),
                    pltpu.make_async_copy(ss_hbm, ss_ref, tbl_sems.at[1])]
    for cp in table_copies:
        cp.start()
    wext[0:n_rows, :] = wukt_ref[...]
    m_s[...] = jnp.zeros(m_s.shape, F32)
    l_s[...] = jnp.zeros(l_s.shape, F32)
    acc_s[...] = jnp.zeros(acc_s.shape, F32)
    for g0 in range(DMA_LOOKAHEAD):
        for cp in copies(g0, g0):
            cp.start()
    for cp in table_copies:
        cp.wait()

    def body(g, carry):
        slot = g % DMA_SLOTS
        b = g // n_steps
        c = g % n_steps
        first = c == 0

        for cp in copies(g, slot):
            cp.wait()

        def rows16(x):
            return jnp.concatenate([x, jnp.zeros((pad_rows - n_heads, x.shape[1]), F32)], axis=0).astype(BF16)

        wext[n_rows:n_rows + pad_rows, :] = rows16(qt_ref[b])
        w = wext[...]
        qa = rows16(qa_ref[b])
        qb = rows16(qb_ref[b])
        subs = range(n_sub)
        rot = []
        for sub in subs:
            krt = rbuf[slot, sub]
            tbl = c * n_sub + sub
            sa = jnp.dot(qa, (krt * cc_ref[tbl]).astype(BF16), preferred_element_type=F32)[0:n_heads]
            sb = jnp.dot(qb, (krt * ss_ref[tbl]).astype(BF16), preferred_element_type=F32)[0:n_heads]
            rot.append((sa + sb, jnp.sum(krt * krt, axis=0, keepdims=True)))
        cbs = [cbuf[slot, sub].astype(BF16) for sub in subs]
        bigs = [_nt_dot(w, cbs[sub]) for sub in subs]

        for cp in copies(jnp.minimum(g + DMA_LOOKAHEAD, n_total - 1), (g + DMA_LOOKAHEAD) % DMA_SLOTS):
            cp.start()

        parts = []
        for sub in subs:
            kt = bigs[sub][0:n_rows].reshape(n_heads, dn, tok)
            ssq = jnp.sum(kt * kt, axis=1)
            sn = bigs[sub][n_rows:n_rows + n_heads]
            sr, krsq = rot[sub]
            s = (sn + sr) * lax.rsqrt((ssq + krsq) / dqk + EPS)
            mc = jnp.max(s, axis=-1, keepdims=True)
            p = jnp.exp(s - mc)
            lc = jnp.sum(p, axis=-1, keepdims=True)
            p16 = jnp.concatenate([p, jnp.zeros((pad_rows - n_heads, tok), F32)], axis=0).astype(BF16)
            pv = jnp.dot(p16, cbs[sub], preferred_element_type=F32)[0:n_heads]
            parts.append((mc, lc, pv))

        m_run = jnp.where(first, sself_ref[b], m_s[...])
        l_run = jnp.where(first, 1.0, l_s[...])
        acc = jnp.where(first, jnp.broadcast_to(cnew_ref[b], acc_s.shape), acc_s[...])
        for mc, lc, pv in parts:
            m_new = jnp.maximum(m_run, mc)
            wa = jnp.exp(m_run - m_new)
            wb = jnp.exp(mc - m_new)
            l_run = l_run * wa + lc * wb
            acc = acc * wa + pv * wb
            m_run = m_new
        m_s[...], l_s[...], acc_s[...] = m_run, l_run, acc
        lat_ref[b] = acc * (1.0 / l_run)
        return carry

    lax.fori_loop(0, trips_ref[0], body, 0)
    for g1 in range(n_total, n_total + DMA_LOOKAHEAD):
        for cp in copies(n_total - 1, g1 % DMA_SLOTS):
            cp.wait()


def _attn_sample(page_table, wukt, qt, qa, qb, sself, cnew, cc3, ss3, cache_ckv, cache_krt,
                 layer, n_sub, pages_per_sub, n_heads, dn, dqk):
    n_req, n_pages = page_table.shape
    page = cache_ckv.shape[2]
    kv_rank = cache_ckv.shape[3]
    rope = cache_krt.shape[2]
    tok = pages_per_sub * page
    pages_per_step = n_sub * pages_per_sub
    n_steps = n_pages // pages_per_step
    assert n_req * n_steps >= DMA_LOOKAHEAD
    pad_rows = 2 * SUBLANES * pl.cdiv(n_heads, 2 * SUBLANES)
    dims = (layer, n_req, n_steps, n_sub, pages_per_sub, page, n_heads, dn, dqk)
    full = lambda *shape: _resident(shape, lambda i, pt, trips: (0,) * len(shape))
    grid_spec = pltpu.PrefetchScalarGridSpec(
        num_scalar_prefetch=2,
        grid=(1,),
        in_specs=[
            full(n_heads * dn, kv_rank),
            full(n_req, n_heads, kv_rank),
            full(n_req, n_heads, rope),
            full(n_req, n_heads, rope),
            full(n_req, n_heads, 1),
            full(n_req, 1, kv_rank),
            pl.BlockSpec(memory_space=pl.ANY),
            pl.BlockSpec(memory_space=pl.ANY),
            pl.BlockSpec(memory_space=pl.ANY),
            pl.BlockSpec(memory_space=pl.ANY),
        ],
        out_specs=pl.BlockSpec((n_req, n_heads, kv_rank), lambda i, pt, trips: (0, 0, 0)),
        scratch_shapes=[
            pltpu.VMEM((n_heads * dn + pad_rows, kv_rank), BF16),
            pltpu.VMEM((DMA_SLOTS, n_sub, tok, kv_rank), F32),
            pltpu.VMEM((DMA_SLOTS, n_sub, rope, tok), F32),
            pltpu.SemaphoreType.DMA((2, DMA_SLOTS, n_sub)),
            pltpu.VMEM((n_heads, 1), F32),
            pltpu.VMEM((n_heads, 1), F32),
            pltpu.VMEM((n_heads, kv_rank), F32),
            pltpu.VMEM(cc3.shape, F32),
            pltpu.VMEM(ss3.shape, F32),
            pltpu.SemaphoreType.DMA((2,)),
        ],
    )
    return pl.pallas_call(
        functools.partial(_attn_sample_kernel, dims=dims),
        grid_spec=grid_spec,
        out_shape=jax.ShapeDtypeStruct((n_req, n_heads, kv_rank), F32),
        compiler_params=_params(("arbitrary",)),
        name="attn_sample",
    )(page_table, jnp.full((1,), n_req * n_steps, jnp.int32), wukt, qt, qa, qb, sself, cnew, cc3, ss3,
      cache_ckv, cache_krt)


def _outproj_kernel(mix_ref, ya_ref, x_ref, g1_ref, sh2_ref, sc2_ref, gffn_ref, wo_ref, x1_ref, h2_ref,
                    *, mix_dim):
    y = jnp.dot(mix_ref[...], wo_ref[0:mix_dim, :], preferred_element_type=F32)
    y = y + jnp.dot(ya_ref[...], wo_ref[mix_dim:, :], preferred_element_type=F32)
    x1 = x_ref[...] + g1_ref[...] * y
    x1_ref[...] = x1
    h2_ref[...] = (_rms(x1, gffn_ref[...]) * (1.0 + sc2_ref[...]) + sh2_ref[...]).astype(BF16)


def _outproj_sample_kernel(mix_ref, lat_ref, wuv_ref, x_ref, g1_ref, sh2_ref, sc2_ref, gffn_ref, wo_ref,
                           x1_ref, h2_ref, *, mix_dim, n_heads, dv):
    y = jnp.dot(mix_ref[...], wo_ref[0:mix_dim, :], preferred_element_type=F32)
    for h in range(n_heads):
        ya = jnp.dot(lat_ref[h].astype(BF16), wuv_ref[h], preferred_element_type=F32).astype(BF16)
        y = y + jnp.dot(ya, wo_ref[mix_dim + h * dv:mix_dim + (h + 1) * dv, :], preferred_element_type=F32)
    x1 = x_ref[...] + g1_ref[...] * y
    x1_ref[...] = x1
    h2_ref[...] = (_rms(x1, gffn_ref[...]) * (1.0 + sc2_ref[...]) + sh2_ref[...]).astype(BF16)


def _outproj(mix, ya, x, mod, g_ffn, w_o, layer, tm, rows_per_batch):
    m, d_model = x.shape
    mix_dim = mix.shape[-1]
    att_dim = ya.shape[-1]
    return pl.pallas_call(
        functools.partial(_outproj_kernel, mix_dim=mix_dim),
        grid=(m // tm,),
        in_specs=[
            pl.BlockSpec((tm, mix_dim), lambda i: (i, 0)),
            pl.BlockSpec((tm, att_dim), lambda i: (i, 0)),
            pl.BlockSpec((tm, d_model), lambda i: (i, 0)),
            _mod_spec(mod, layer, 2, tm, rows_per_batch),
            _mod_spec(mod, layer, 3, tm, rows_per_batch),
            _mod_spec(mod, layer, 4, tm, rows_per_batch),
            pl.BlockSpec((None, 1, d_model), lambda i: (layer, 0, 0)),
            _resident((None, mix_dim + att_dim, d_model), lambda i: (layer, 0, 0)),
        ],
        out_specs=[
            pl.BlockSpec((tm, d_model), lambda i: (i, 0)),
            pl.BlockSpec((tm, d_model), lambda i: (i, 0)),
        ],
        out_shape=[
            jax.ShapeDtypeStruct((m, d_model), F32),
            jax.ShapeDtypeStruct((m, d_model), BF16),
        ],
        compiler_params=_params(("arbitrary",)),
        name="outproj",
    )(mix, ya, x, mod, mod, mod, g_ffn, w_o)


def _outproj_sample(mix, lat_t, w_uv, x, mod, g_ffn, w_o, layer):
    m, d_model = x.shape
    mix_dim = mix.shape[-1]
    n_heads, _, kv_rank = lat_t.shape
    dv = w_uv.shape[-1]
    return pl.pallas_call(
        functools.partial(_outproj_sample_kernel, mix_dim=mix_dim, n_heads=n_heads, dv=dv),
        grid=(1,),
        in_specs=[
            pl.BlockSpec((m, mix_dim), lambda i: (0, 0)),
            pl.BlockSpec((n_heads, m, kv_rank), lambda i: (0, 0, 0)),
            pl.BlockSpec((None, n_heads, kv_rank, dv), lambda i: (layer, 0, 0, 0)),
            pl.BlockSpec((m, d_model), lambda i: (0, 0)),
            _mod_spec(mod, layer, 2, m, 1),
            _mod_spec(mod, layer, 3, m, 1),
            _mod_spec(mod, layer, 4, m, 1),
            pl.BlockSpec((None, 1, d_model), lambda i: (layer, 0, 0)),
            _resident((None, mix_dim + n_heads * dv, d_model), lambda i: (layer, 0, 0)),
        ],
        out_specs=[
            pl.BlockSpec((m, d_model), lambda i: (0, 0)),
            pl.BlockSpec((m, d_model), lambda i: (0, 0)),
        ],
        out_shape=[
            jax.ShapeDtypeStruct((m, d_model), F32),
            jax.ShapeDtypeStruct((m, d_model), BF16),
        ],
        compiler_params=_params(("arbitrary",)),
        name="outproj_sample",
    )(mix, lat_t, w_uv, x, mod, mod, mod, g_ffn, w_o)


def _ffn_kernel(h_ref, x1_ref, g2_ref, w1_ref, w3_ref, w2_ref, o_ref, acc_s, *, n_split):
    k = pl.program_id(1)

    @pl.when(k == 0)
    def _():
        acc_s[...] = jnp.zeros(acc_s.shape, F32)

    h = h_ref[...]
    hf = w1_ref.shape[-1] // n_split
    ab = [(jnp.dot(h, w1_ref[:, t * hf:(t + 1) * hf], preferred_element_type=F32),
           jnp.dot(h, w3_ref[:, t * hf:(t + 1) * hf], preferred_element_type=F32)) for t in range(n_split)]
    for t, (a, b) in enumerate(ab):
        f = (a * jax.nn.sigmoid(a) * b).astype(BF16)
        acc_s[...] += jnp.dot(f, w2_ref[t * hf:(t + 1) * hf, :], preferred_element_type=F32)

    @pl.when(k == pl.num_programs(1) - 1)
    def _():
        o_ref[...] = x1_ref[...] + g2_ref[...] * acc_s[...]


def _ffn(h2, x1, mod, w1, w3, w2, layer, tm, fc, rows_per_batch):
    m, d_model = x1.shape
    d_ff = w1.shape[-1]
    mod_spec = _mod_spec(mod, layer, 5, tm, rows_per_batch)
    mod_map = mod_spec.index_map
    return pl.pallas_call(
        functools.partial(_ffn_kernel, n_split=FFN_SPLIT if fc % (FFN_SPLIT * LANES) == 0 else 1),
        grid=(m // tm, d_ff // fc),
        in_specs=[
            pl.BlockSpec((tm, d_model), lambda i, k: (i, 0)),
            pl.BlockSpec((tm, d_model), lambda i, k: (i, 0)),
            pl.BlockSpec(mod_spec.block_shape, lambda i, k: mod_map(i)),
            pl.BlockSpec((None, d_model, fc), lambda i, k: (layer, 0, k)),
            pl.BlockSpec((None, d_model, fc), lambda i, k: (layer, 0, k)),
            pl.BlockSpec((None, fc, d_model), lambda i, k: (layer, k, 0)),
        ],
        out_specs=pl.BlockSpec((tm, d_model), lambda i, k: (i, 0)),
        out_shape=jax.ShapeDtypeStruct((m, d_model), F32),
        scratch_shapes=[pltpu.VMEM((tm, d_model), F32)],
        compiler_params=_params(("arbitrary", "arbitrary")),
        name="ffn",
    )(h2, x1, mod, w1, w3, w2)


def _spread_last(x, half):
    pad = [(0, 0)] * (x.ndim - 1) + [(0, ROPE_HALF_OFF - half)]
    return jnp.concatenate([jnp.pad(x[..., :half], pad), jnp.pad(x[..., half:], pad)], axis=-1)


def _compact_last(x, half):
    return jnp.concatenate([x[..., :half], x[..., ROPE_HALF_OFF:ROPE_HALF_OFF + half]], axis=-1)


def _rope_tables(pos, half):
    inv = ROPE_THETA ** (-jnp.arange(half, dtype=F32) / half)
    ang = pos.astype(F32)[:, None] * inv[None, :]
    return jnp.cos(ang), jnp.sin(ang)


def kernel(x_prompt, x_sample, cache_ckv, cache_kr, state_conv, state_pool, page_table,
           c_prompt, c_sample, g_mix, w_ada, b_ada, w_in, w_conv, w_pool, s_pool,
           g_cq, g_ckv, w_uq, w_uk, w_uv, g_q, g_k, w_o, g_ffn, w1, w3, w2):
    n_batch, n_tok, d_model = x_prompt.shape
    n_req, dec_seq, _ = x_sample.shape
    assert dec_seq == 1, "the sample group carries one new token per request"
    n_layers = w_in.shape[0]
    conv_w, conv_dim = w_conv.shape[1:]
    pool_dim = s_pool.shape[1]
    pool_ctx = state_pool.shape[2]
    assert pool_ctx == max(POOL_WINDOWS) - 1 and w_pool.shape[1] == len(POOL_WINDOWS)
    q_rank = g_cq.shape[1]
    kv_rank = g_ckv.shape[1]
    n_heads, dqk = w_uq.shape[2:]
    dn = w_uk.shape[3]
    rope = dqk - dn
    half = rope // 2
    dv = w_uv.shape[3]
    page = cache_ckv.shape[2]
    n_pages = page_table.shape[1]
    past = n_pages * page
    base = w_in.shape[2] - rope
    assert base == 3 * conv_dim + pool_dim + q_rank + kv_rank and half <= ROPE_HALF_OFF
    assert dn % LANES == 0 and n_tok >= 2 * SUBLANES
    cfg = (conv_dim, pool_dim, q_rank, kv_rank, n_heads, dn, dqk, conv_w, pool_ctx)

    w_in_b = w_in.astype(BF16)
    w_kr = _spread_last(w_in[..., base:], half).astype(BF16)
    w_uq_p = jnp.concatenate([w_uq[..., :dn], _spread_last(w_uq[..., dn:], half)], axis=-1)
    w_uq_p = w_uq_p.reshape(n_layers, q_rank, n_heads * (dn + ROPE_SPREAD)).astype(BF16)
    g_q_p = jnp.concatenate([g_q[:, :dn], _spread_last(g_q[:, dn:], half)], axis=-1)[:, None, :]
    g_kr = _spread_last(g_k[:, dn:], half)[:, None, :]
    g_krb = _spread_last(jnp.concatenate([g_k[:, dn:dn + half], -g_k[:, dn + half:]], axis=-1), half)[:, None, :]
    w_uk_b = w_uk.reshape(n_layers, kv_rank, n_heads * dn).astype(BF16)
    lw = {
        "w_conv": w_conv, "w_pool": w_pool.astype(BF16), "s_pool": s_pool[:, None, :],
        "g_cq": g_cq[:, None, :], "g_ckv": g_ckv[:, None, :], "w_uq": w_uq_p, "w_uk": w_uk_b,
        "g_q": g_q_p, "g_kn": g_k[:, None, :dn], "g_kr": g_kr, "g_krb": g_krb,
    }
    w_ukt = jnp.swapaxes(w_uk_b, 1, 2)
    w_uv_b = jnp.transpose(w_uv, (0, 2, 1, 3)).astype(BF16)
    w_o_b = w_o.astype(BF16)
    w1_b, w3_b, w2_b = w1.astype(BF16), w3.astype(BF16), w2.astype(BF16)
    g_mix3, g_ffn3 = g_mix[:, None, :], g_ffn[:, None, :]

    cos_p, sin_p = _rope_tables(jnp.arange(n_tok), half)
    cc_p = _spread_last(jnp.concatenate([cos_p, cos_p], axis=-1), half)
    ss_p = _spread_last(jnp.concatenate([-sin_p, sin_p], axis=-1), half)
    cos_s, sin_s = _rope_tables(past + jnp.arange(dec_seq), half)
    cc_s = _spread_last(jnp.concatenate([cos_s, cos_s], axis=-1), half)
    ss_s = _spread_last(jnp.concatenate([-sin_s, sin_s], axis=-1), half)
    tm_p = math.gcd(n_tok, 512)
    tc_p = math.gcd(n_tok, 512)
    tq_p = math.gcd(n_tok, 256)
    tk_p = 2 * tq_p if n_tok % (2 * tq_p) == 0 else tq_p
    fc = math.gcd(w1.shape[2], 512)
    pages_per_sub = math.gcd(n_pages, 4)
    n_sub = math.gcd(n_pages // pages_per_sub, 8)

    cos_c, sin_c = _rope_tables(jnp.arange(past), half)
    sub_tok = pages_per_sub * page
    by_sub = lambda t: jnp.transpose(t.reshape(past // sub_tok, sub_tok, rope), (0, 2, 1))
    cc_c = by_sub(jnp.concatenate([cos_c, cos_c], axis=-1))
    ss_c = by_sub(jnp.concatenate([sin_c, sin_c], axis=-1))
    cache_krt = jnp.swapaxes(cache_kr, 2, 3)

    mod = _ada_mod(jnp.concatenate([c_prompt, c_sample], axis=0), w_ada, b_ada, math.gcd(d_model, 1024))
    mod_p = mod[:, :, :n_batch, None, :]
    mod_s = mod[:, :, n_batch:, :]

    st_conv = jnp.swapaxes(state_conv, 1, 2)
    st_pool = jnp.swapaxes(state_pool, 1, 2)

    xp = x_prompt.reshape(n_batch * n_tok, d_model)
    xs = x_sample.reshape(n_req, d_model)
    outs = {k: [] for k in ("ckv_p", "kr_p", "conv_p", "pool_p", "ckv_s", "kr_s", "conv_s", "pool_s")}
    for l in range(n_layers):
        z = _inproj(xp, mod_p, g_mix3, w_in_b, w_kr, base, l, tm_p, n_tok)
        mix, q, k, ckvn, vb, kr_sp, cst, pst = _mix_prompt(z, cc_p, ss_p, lw, l, n_batch, n_tok, tc_p, cfg)
        ya = _attn_prompt(q, k, vb, w_uv_b, l, tq_p, tk_p)
        x1, h2 = _outproj(mix, ya, xp, mod_p, g_ffn3, w_o_b, l, tm_p, n_tok)
        xp = _ffn(h2, x1, mod_p, w1_b, w3_b, w2_b, l, tm_p, fc, n_tok)
        outs["ckv_p"].append(ckvn)
        outs["kr_p"].append(_compact_last(kr_sp, half))
        outs["conv_p"].append(cst)
        outs["pool_p"].append(pst)

        z = _inproj(xs, mod_s, g_mix3, w_in_b, w_kr, base, l, n_req, 1)
        mix, u_new, ckv_new, qt, qa, qb, sself = _mix_sample(z, st_conv, st_pool, cc_s, ss_s, lw, l, cfg, past)
        by_req = lambda a: jnp.swapaxes(a, 0, 1)
        lat = _attn_sample(page_table, w_ukt[l], by_req(qt), by_req(_compact_last(qa, half)),
                           by_req(_compact_last(qb, half)), sself[:, :, None], ckv_new[:, None, :],
                           cc_c, ss_c, cache_ckv, cache_krt, l, n_sub, pages_per_sub, n_heads, dn, dqk)
        x1, h2 = _outproj_sample(mix, jnp.swapaxes(lat, 0, 1), w_uv_b, xs, mod_s, g_ffn3, w_o_b, l)
        xs = _ffn(h2, x1, mod_s, w1_b, w3_b, w2_b, l, n_req, fc, 1)
        outs["ckv_s"].append(ckv_new[:, None, :])
        outs["kr_s"].append(_compact_last(z[:, base:], half)[:, None, :])
        outs["conv_s"].append(jnp.concatenate([state_conv[l][:, 1:], u_new[:, None, :]], axis=1))
        outs["pool_s"].append(jnp.concatenate([state_pool[l][:, 1:], z[:, None, 3 * conv_dim:3 * conv_dim + pool_dim]], axis=1))

    st = {k: jnp.stack(v) for k, v in outs.items()}
    return (xp.reshape(n_batch, n_tok, d_model), xs.reshape(n_req, dec_seq, d_model),
            st["ckv_p"], st["kr_p"], st["conv_p"], st["pool_p"],
            st["ckv_s"], st["kr_s"], st["conv_s"], st["pool_s"])
```

```python
import functools
import math

import jax
import jax.numpy as jnp
from jax import lax
from jax.experimental import pallas as pl
from jax.experimental.pallas import tpu as pltpu

F32 = jnp.float32
BF16 = jnp.bfloat16

EPS = 1e-6
ROPE_THETA = 10000.0
POOL_WINDOWS = (2, 4, 8, 16)

LANES = 128
SUBLANES = 8
VMEM_BYTES_V7X = 64 * 1024 * 1024
VMEM_LIMIT = VMEM_BYTES_V7X - 8 * 1024 * 1024

ROPE_SPREAD = LANES
ROPE_HALF_OFF = LANES // 2
HEAD_SKEW = 1
FFN_SPLIT = 2
DMA_LOOKAHEAD = 2
DMA_SLOTS = DMA_LOOKAHEAD + 1


def _params(semantics):
    return pltpu.CompilerParams(dimension_semantics=semantics, vmem_limit_bytes=VMEM_LIMIT)


def _resident(block_shape, index_map):
    return pl.BlockSpec(block_shape, index_map, pipeline_mode=pl.Buffered(1))


def _rms(x, g):
    return x * lax.rsqrt(jnp.mean(x * x, axis=-1, keepdims=True) + EPS) * g


def _nt_dot(a, b):
    return lax.dot_general(a, b, (((1,), (1,)), ((), ())), preferred_element_type=F32)


def _rope_spread(x, cc, ss):
    return x * cc + pltpu.roll(x, ROPE_HALF_OFF, 1) * ss


def _ada_kernel(c_ref, w_ref, b_ref, o_ref):
    c = c_ref[...]
    a = (c * jax.nn.sigmoid(c)).astype(BF16)
    o_ref[...] = jnp.dot(a, w_ref[...].astype(BF16), preferred_element_type=F32) + b_ref[...]


def _ada_mod(c_all, w_ada, b_ada, tn):
    n_layers, d_model, six_d = w_ada.shape
    rows = c_all.shape[0]
    n_per = d_model // tn
    return pl.pallas_call(
        _ada_kernel,
        grid=(n_layers, 6, n_per),
        in_specs=[
            pl.BlockSpec((rows, d_model), lambda l, k, n: (0, 0)),
            pl.BlockSpec((None, d_model, tn), lambda l, k, n: (l, 0, k * n_per + n)),
            pl.BlockSpec((None, 1, tn), lambda l, k, n: (l, 0, k * n_per + n)),
        ],
        out_specs=pl.BlockSpec((None, None, rows, tn), lambda l, k, n: (l, k, 0, n)),
        out_shape=jax.ShapeDtypeStruct((n_layers, 6, rows, d_model), F32),
        compiler_params=_params(("arbitrary", "arbitrary", "arbitrary")),
        name="ada_mod",
    )(c_all, w_ada, b_ada.reshape(n_layers, 1, six_d))


def _mod_spec(mod, layer, chunk, tm, rows_per_batch):
    if mod.ndim == 5:
        tiles_per_batch = rows_per_batch // tm
        return pl.BlockSpec((None, None, None, 1, mod.shape[-1]),
                            lambda i: (layer, chunk, i // tiles_per_batch, 0, 0))
    return pl.BlockSpec((None, None, tm, mod.shape[-1]), lambda i: (layer, chunk, i, 0))


def _inproj_kernel(x_ref, sh_ref, sc_ref, g_ref, w_ref, wkr_ref, z_ref):
    h = (_rms(x_ref[...], g_ref[...]) * (1.0 + sc_ref[...]) + sh_ref[...]).astype(BF16)
    n_main = w_ref.shape[-1]
    z_ref[:, 0:n_main] = jnp.dot(h, w_ref[...], preferred_element_type=F32)
    z_ref[:, n_main:] = jnp.dot(h, wkr_ref[...], preferred_element_type=F32)


def _inproj(x, mod, g_mix, w_in_b, w_kr, n_main, layer, tm, rows_per_batch):
    m, d_model = x.shape
    assert n_main % LANES == 0
    n_out = n_main + w_kr.shape[-1]
    return pl.pallas_call(
        _inproj_kernel,
        grid=(m // tm,),
        in_specs=[
            pl.BlockSpec((tm, d_model), lambda i: (i, 0)),
            _mod_spec(mod, layer, 0, tm, rows_per_batch),
            _mod_spec(mod, layer, 1, tm, rows_per_batch),
            pl.BlockSpec((None, 1, d_model), lambda i: (layer, 0, 0)),
            _resident((None, d_model, n_main), lambda i: (layer, 0, 0)),
            _resident((None, d_model, w_kr.shape[-1]), lambda i: (layer, 0, 0)),
        ],
        out_specs=pl.BlockSpec((tm, n_out), lambda i: (i, 0)),
        out_shape=jax.ShapeDtypeStruct((m, n_out), F32),
        compiler_params=_params(("arbitrary",)),
        name="inproj",
    )(x, mod, mod, g_mix, w_in_b, w_kr)


def _queries(cq, gcq, wuq, gq, cc, ss, n_heads, dn, dqk, scale):
    cqn = _rms(cq, gcq).astype(BF16)
    qall = jnp.dot(cqn, wuq, preferred_element_type=F32)
    hw = dn + ROPE_SPREAD
    out = []
    for h in range(n_heads):
        qn = qall[:, h * hw:h * hw + dn]
        qr = qall[:, h * hw + dn:(h + 1) * hw]
        ssq = jnp.sum(qn * qn, axis=-1, keepdims=True) + jnp.sum(qr * qr, axis=-1, keepdims=True)
        r = lax.rsqrt(ssq / dqk + EPS) * scale
        qn = qn * r * gq[:, :dn]
        qr = _rope_spread(qr * r * gq[:, dn:], cc, ss)
        out.append((qn, qr))
    return out


def _keys(ckvn_b, kr, wuk, gkn, gkr, cc, ss, n_heads, dn, dqk):
    kn_all = jnp.dot(ckvn_b, wuk, preferred_element_type=F32)
    krsq = jnp.sum(kr * kr, axis=-1, keepdims=True)
    krot = _rope_spread(kr * gkr, cc, ss)
    out = []
    for h in range(n_heads):
        kn = kn_all[:, h * dn:(h + 1) * dn]
        r = lax.rsqrt((jnp.sum(kn * kn, axis=-1, keepdims=True) + krsq) / dqk + EPS)
        out.append((kn * r * gkn, krot * r))
    return out


def _mix_prompt_kernel(z_ref, cc_ref, ss_ref, wconv_ref, wpool_ref, spool_ref, gcq_ref, gckv_ref,
                       wuq_ref, wuk_ref, gq_ref, gkn_ref, gkr_ref,
                       mix_ref, q_ref, k_ref, ckv_ref, vb_ref, kr_ref, cst_ref, pst_ref,
                       ubuf, vbuf, *, dims):
    (tc, conv_dim, pool_dim, q_rank, kv_rank, n_heads, dn, dqk, scale, conv_w, pool_ctx) = dims
    j = pl.program_id(1)
    halo_u = SUBLANES
    halo_v = 2 * SUBLANES

    @pl.when(j == 0)
    def _():
        ubuf[0:halo_u, :] = jnp.zeros((halo_u, conv_dim), F32)
        vbuf[0:halo_v, :] = jnp.zeros((halo_v, pool_dim), F32)

    o = 0
    hc = z_ref[:, o:o + conv_dim]; o += conv_dim
    bc = z_ref[:, o:o + conv_dim]; o += conv_dim
    cg = z_ref[:, o:o + conv_dim]; o += conv_dim
    vp = z_ref[:, o:o + pool_dim]; o += pool_dim
    cq = z_ref[:, o:o + q_rank]; o += q_rank
    ckv = z_ref[:, o:o + kv_rank]; o += kv_rank
    kr = z_ref[:, o:o + ROPE_SPREAD]

    u = cg * hc
    ubuf[halo_u:halo_u + tc, :] = u
    acc = wconv_ref[conv_w - 1:conv_w, :] * u
    for t in range(1, conv_w):
        acc = acc + wconv_ref[conv_w - 1 - t:conv_w - t, :] * ubuf[pl.ds(halo_u - t, tc), :]
    mix_ref[:, 0:conv_dim] = (bc * acc).astype(BF16)
    cst_ref[...] = ubuf[pl.ds(halo_u + tc - (conv_w - 1), conv_w - 1), :]
    ubuf[0:halo_u, :] = ubuf[tc:tc + halo_u, :]

    vbuf[halo_v:halo_v + tc, :] = vp
    pg = pool_dim // len(POOL_WINDOWS)
    pos = j * tc + lax.broadcasted_iota(jnp.int32, (tc, pg), 0)
    for g, w in enumerate(POOL_WINDOWS):
        sl = slice(g * pg, (g + 1) * pg)
        v = vp[:, sl]
        s = v
        for t in range(1, w):
            s = s + vbuf[pl.ds(halo_v - t, tc), sl]
        cnt = jnp.minimum(pos + 1, w).astype(F32)
        d = s / cnt - v
        y = jnp.dot(d.astype(BF16), wpool_ref[g], preferred_element_type=F32) * spool_ref[:, sl]
        mix_ref[:, conv_dim + g * pg:conv_dim + (g + 1) * pg] = y.astype(BF16)
    pst_ref[...] = vbuf[pl.ds(halo_v + tc - pool_ctx, pool_ctx), :]
    vbuf[0:halo_v, :] = vbuf[tc:tc + halo_v, :]

    cc = cc_ref[...]
    ss = ss_ref[...]
    qs = _queries(cq, gcq_ref[...], wuq_ref[...], gq_ref[...], cc, ss, n_heads, dn, dqk, scale)
    for h, (qn, qr) in enumerate(qs):
        q_ref[h, :, 0:dn] = qn.astype(BF16)
        q_ref[h, :, dn:dn + ROPE_SPREAD] = qr.astype(BF16)
    ckvn = _rms(ckv, gckv_ref[...])
    ckv_ref[...] = ckvn
    ckvn_b = ckvn.astype(BF16)
    vb_ref[...] = ckvn_b
    kr_ref[...] = kr
    ks = _keys(ckvn_b, kr, wuk_ref[...], gkn_ref[...], gkr_ref[...], cc, ss, n_heads, dn, dqk)
    for h, (kn, krot) in enumerate(ks):
        k_ref[h, :, 0:dn] = kn.astype(BF16)
        k_ref[h, :, dn:dn + ROPE_SPREAD] = krot.astype(BF16)


def _mix_prompt(z, cc, ss, lw, layer, n_batch, n_tok, tc, cfg):
    (conv_dim, pool_dim, q_rank, kv_rank, n_heads, dn, dqk, conv_w, pool_ctx) = cfg
    n_in = z.shape[-1]
    hw = dn + ROPE_SPREAD
    tiles = n_tok // tc
    dims = (tc, conv_dim, pool_dim, q_rank, kv_rank, n_heads, dn, dqk, dqk ** -0.5 * math.log2(math.e),
            conv_w, pool_ctx)
    lay = lambda *shape: pl.BlockSpec((None,) + shape, lambda b, j: (layer,) + (0,) * len(shape))
    n_groups = len(POOL_WINDOWS)
    pg = pool_dim // n_groups
    mix_dim = conv_dim + pool_dim
    return pl.pallas_call(
        functools.partial(_mix_prompt_kernel, dims=dims),
        grid=(n_batch, tiles),
        in_specs=[
            pl.BlockSpec((tc, n_in), lambda b, j: (b * tiles + j, 0)),
            pl.BlockSpec((tc, ROPE_SPREAD), lambda b, j: (j, 0)),
            pl.BlockSpec((tc, ROPE_SPREAD), lambda b, j: (j, 0)),
            lay(conv_w, conv_dim),
            lay(n_groups, pg, pg),
            lay(1, pool_dim),
            lay(1, q_rank),
            lay(1, kv_rank),
            lay(q_rank, n_heads * hw),
            lay(kv_rank, n_heads * dn),
            lay(1, hw),
            lay(1, dn),
            lay(1, ROPE_SPREAD),
        ],
        out_specs=[
            pl.BlockSpec((tc, mix_dim), lambda b, j: (b * tiles + j, 0)),
            pl.BlockSpec((None, n_heads, tc, hw), lambda b, j: (b, 0, j, 0)),
            pl.BlockSpec((None, n_heads, tc, hw), lambda b, j: (b, 0, j, 0)),
            pl.BlockSpec((None, tc, kv_rank), lambda b, j: (b, j, 0)),
            pl.BlockSpec((None, tc, kv_rank), lambda b, j: (b, j, 0)),
            pl.BlockSpec((None, tc, ROPE_SPREAD), lambda b, j: (b, j, 0)),
            pl.BlockSpec((None, conv_w - 1, conv_dim), lambda b, j: (b, 0, 0)),
            pl.BlockSpec((None, pool_ctx, pool_dim), lambda b, j: (b, 0, 0)),
        ],
        out_shape=[
            jax.ShapeDtypeStruct((n_batch * n_tok, mix_dim), BF16),
            jax.ShapeDtypeStruct((n_batch, n_heads, n_tok, hw), BF16),
            jax.ShapeDtypeStruct((n_batch, n_heads, n_tok, hw), BF16),
            jax.ShapeDtypeStruct((n_batch, n_tok, kv_rank), F32),
            jax.ShapeDtypeStruct((n_batch, n_tok, kv_rank), BF16),
            jax.ShapeDtypeStruct((n_batch, n_tok, ROPE_SPREAD), F32),
            jax.ShapeDtypeStruct((n_batch, conv_w - 1, conv_dim), F32),
            jax.ShapeDtypeStruct((n_batch, pool_ctx, pool_dim), F32),
        ],
        scratch_shapes=[
            pltpu.VMEM((SUBLANES + tc, conv_dim), F32),
            pltpu.VMEM((2 * SUBLANES + tc, pool_dim), F32),
        ],
        compiler_params=_params(("arbitrary", "arbitrary")),
        name="mix_prompt",
    )(z, cc, ss, lw["w_conv"], lw["w_pool"], lw["s_pool"], lw["g_cq"], lw["g_ckv"],
      lw["w_uq"], lw["w_uk"], lw["g_q"], lw["g_kn"], lw["g_kr"])


def _mix_sample_kernel(z_ref, cst_ref, pst_ref, cc_ref, ss_ref, wconv_ref, wpool_ref, spool_ref,
                       gcq_ref, gckv_ref, wuq_ref, wuk_ref, gq_ref, gkn_ref, gkr_ref, gkrb_ref,
                       mix_ref, u_ref, ckv_ref, qt_ref, qa_ref, qb_ref, sself_ref, *, dims):
    (conv_dim, pool_dim, q_rank, kv_rank, n_heads, dn, dqk, scale, conv_w, pool_ctx, past) = dims
    o = 0
    hc = z_ref[:, o:o + conv_dim]; o += conv_dim
    bc = z_ref[:, o:o + conv_dim]; o += conv_dim
    cg = z_ref[:, o:o + conv_dim]; o += conv_dim
    vp = z_ref[:, o:o + pool_dim]; o += pool_dim
    cq = z_ref[:, o:o + q_rank]; o += q_rank
    ckv = z_ref[:, o:o + kv_rank]; o += kv_rank
    kr = z_ref[:, o:o + ROPE_SPREAD]

    u = cg * hc
    u_ref[...] = u
    acc = wconv_ref[conv_w - 1:conv_w, :] * u
    for t in range(1, conv_w):
        acc = acc + wconv_ref[conv_w - 1 - t:conv_w - t, :] * cst_ref[conv_w - 1 - t]
    mix_ref[:, 0:conv_dim] = (bc * acc).astype(BF16)

    pg = pool_dim // len(POOL_WINDOWS)
    for g, w in enumerate(POOL_WINDOWS):
        sl = slice(g * pg, (g + 1) * pg)
        v = vp[:, sl]
        s = v
        for t in range(1, w):
            s = s + pst_ref[pool_ctx - t, :, sl]
        d = s / float(min(past + 1, w)) - v
        y = jnp.dot(d.astype(BF16), wpool_ref[g], preferred_element_type=F32) * spool_ref[:, sl]
        mix_ref[:, conv_dim + g * pg:conv_dim + (g + 1) * pg] = y.astype(BF16)

    cc = cc_ref[...]
    ss = ss_ref[...]
    qs = _queries(cq, gcq_ref[...], wuq_ref[...], gq_ref[...], cc, ss, n_heads, dn, dqk, scale)
    ckvn = _rms(ckv, gckv_ref[...])
    ckv_ref[...] = ckvn
    wuk = wuk_ref[...]
    gkn = gkn_ref[...]
    ks = _keys(ckvn.astype(BF16), kr, wuk, gkn, gkr_ref[...], cc, ss, n_heads, dn, dqk)
    for h in range(n_heads):
        qn, qr = qs[h]
        kn, krot = ks[h]
        sself_ref[:, h:h + 1] = (jnp.sum(qn * kn, axis=-1, keepdims=True)
                                 + jnp.sum(qr * krot, axis=-1, keepdims=True))
        qt_ref[h] = _nt_dot((qn * gkn).astype(BF16), wuk[:, h * dn:(h + 1) * dn])
        qa_ref[h] = qr * gkr_ref[...]
        qb_ref[h] = pltpu.roll(qr, ROPE_HALF_OFF, 1) * gkrb_ref[...]


def _mix_sample(z, cst, pst, cc, ss, lw, layer, cfg, past):
    (conv_dim, pool_dim, q_rank, kv_rank, n_heads, dn, dqk, conv_w, pool_ctx) = cfg
    rows, n_in = z.shape
    hw = dn + ROPE_SPREAD
    dims = (conv_dim, pool_dim, q_rank, kv_rank, n_heads, dn, dqk, dqk ** -0.5, conv_w, pool_ctx, past)
    lay = lambda *shape: pl.BlockSpec((None,) + shape, lambda i: (layer,) + (0,) * len(shape))
    full = lambda *shape: pl.BlockSpec(shape, lambda i: (0,) * len(shape))
    n_groups = len(POOL_WINDOWS)
    pg = pool_dim // n_groups
    mix_dim = conv_dim + pool_dim
    return pl.pallas_call(
        functools.partial(_mix_sample_kernel, dims=dims),
        grid=(1,),
        in_specs=[
            full(rows, n_in),
            lay(conv_w - 1, rows, conv_dim),
            lay(pool_ctx, rows, pool_dim),
            full(1, ROPE_SPREAD),
            full(1, ROPE_SPREAD),
            lay(conv_w, conv_dim),
            lay(n_groups, pg, pg),
            lay(1, pool_dim),
            lay(1, q_rank),
            lay(1, kv_rank),
            lay(q_rank, n_heads * hw),
            lay(kv_rank, n_heads * dn),
            lay(1, hw),
            lay(1, dn),
            lay(1, ROPE_SPREAD),
            lay(1, ROPE_SPREAD),
        ],
        out_specs=[
            full(rows, mix_dim),
            full(rows, conv_dim),
            full(rows, kv_rank),
            full(n_heads, rows, kv_rank),
            full(n_heads, rows, ROPE_SPREAD),
            full(n_heads, rows, ROPE_SPREAD),
            full(rows, n_heads),
        ],
        out_shape=[
            jax.ShapeDtypeStruct((rows, mix_dim), BF16),
            jax.ShapeDtypeStruct((rows, conv_dim), F32),
            jax.ShapeDtypeStruct((rows, kv_rank), F32),
            jax.ShapeDtypeStruct((n_heads, rows, kv_rank), F32),
            jax.ShapeDtypeStruct((n_heads, rows, ROPE_SPREAD), F32),
            jax.ShapeDtypeStruct((n_heads, rows, ROPE_SPREAD), F32),
            jax.ShapeDtypeStruct((rows, n_heads), F32),
        ],
        compiler_params=_params(("arbitrary",)),
        name="mix_sample",
    )(z, cst, pst, cc, ss, lw["w_conv"], lw["w_pool"], lw["s_pool"], lw["g_cq"], lw["g_ckv"],
      lw["w_uq"], lw["w_uk"], lw["g_q"], lw["g_kn"], lw["g_kr"], lw["g_krb"])


def _attn_prompt_kernel(q_ref, k_ref, v_ref, wuv_ref, o_ref, m_s, l_s, acc_s, *, tq, tk, n_heads, dv):
    i = pl.program_id(1)
    acc_tiles = acc_s.shape[-1] // LANES
    m_s[...] = jnp.full(m_s.shape, -jnp.inf, F32)
    l_s[...] = jnp.zeros(l_s.shape, F32)
    acc_s[...] = jnp.zeros(acc_s.shape, F32)

    def step(start, width, masked):
        v = v_ref[pl.ds(start, width), :]

        def scores(h):
            s = _nt_dot(q_ref[h], k_ref[h, pl.ds(start, width), :])
            if masked:
                row = lax.broadcasted_iota(jnp.int32, s.shape, 0)
                col = lax.broadcasted_iota(jnp.int32, s.shape, 1)
                s = jnp.where(col <= row, s, -jnp.inf)
            return s

        def update(h, s):
            m_prev = m_s[h]
            m_new = jnp.maximum(m_prev, jnp.max(s, axis=-1, keepdims=True))
            corr = jnp.exp2(m_prev - m_new)
            ps = [jnp.exp2(s[:, t * LANES:(t + 1) * LANES] - m_new) for t in range(width // LANES)]
            l_s[h] = l_s[h] * corr + functools.reduce(lambda a, b: a + b, ps)
            p = jnp.concatenate(ps, axis=1).astype(BF16)
            pv = jnp.dot(p, v, preferred_element_type=F32)
            acc_s[h] = acc_s[h] * jnp.concatenate([corr] * acc_tiles, axis=1) + pv
            m_s[h] = m_new

        pending = {}
        for t in range(n_heads + HEAD_SKEW):
            if t < n_heads:
                pending[t] = scores(t)
            if t >= HEAD_SKEW:
                update(t - HEAD_SKEW, pending.pop(t - HEAD_SKEW))

    ratio = tk // tq

    def body(j, carry):
        step(pl.multiple_of(j * tk, tk), tk, False)
        return carry

    lax.fori_loop(0, i // ratio, body, 0)
    for r in range(1, ratio):
        @pl.when(i % ratio >= r)
        def _():
            step(pl.multiple_of((i // ratio) * tk + (r - 1) * tq, tq), tq, False)
    step(pl.multiple_of(i * tq, tq), tq, True)
    for h in range(n_heads):
        inv_l = 1.0 / jnp.sum(l_s[h], axis=-1, keepdims=True)
        lat = (acc_s[h] * inv_l).astype(BF16)
        o_ref[:, h * dv:(h + 1) * dv] = jnp.dot(lat, wuv_ref[h], preferred_element_type=F32).astype(o_ref.dtype)


def _attn_prompt(q, k, vb, w_uv, layer, tq, tk):
    n_batch, n_heads, n_tok, hw = q.shape
    kv_rank = vb.shape[-1]
    dv = w_uv.shape[-1]
    tiles = n_tok // tq
    return pl.pallas_call(
        functools.partial(_attn_prompt_kernel, tq=tq, tk=tk, n_heads=n_heads, dv=dv),
        grid=(n_batch, tiles),
        in_specs=[
            pl.BlockSpec((None, n_heads, tq, hw), lambda b, i: (b, 0, i, 0)),
            _resident((None, n_heads, n_tok, hw), lambda b, i: (b, 0, 0, 0)),
            _resident((None, n_tok, kv_rank), lambda b, i: (b, 0, 0)),
            _resident((None, n_heads, kv_rank, dv), lambda b, i: (layer, 0, 0, 0)),
        ],
        out_specs=pl.BlockSpec((tq, n_heads * dv), lambda b, i: (b * tiles + i, 0)),
        out_shape=jax.ShapeDtypeStruct((n_batch * n_tok, n_heads * dv), BF16),
        scratch_shapes=[
            pltpu.VMEM((n_heads, tq, LANES), F32),
            pltpu.VMEM((n_heads, tq, LANES), F32),
            pltpu.VMEM((n_heads, tq, kv_rank), F32),
        ],
        compiler_params=_params(("arbitrary", "arbitrary")),
        name="attn_prompt",
    )(q, k, vb, w_uv)


def _attn_sample_kernel(pt_ref, trips_ref, wukt_ref, qt_ref, qa_ref, qb_ref, sself_ref, cnew_ref, cc_hbm, ss_hbm,
                        cache_ckv, cache_krt, lat_ref,
                        wext, cbuf, rbuf, sems, m_s, l_s, acc_s, cc_ref, ss_ref, tbl_sems, *, dims):
    (layer, n_req, n_steps, n_sub, pages_per_sub, page, n_heads, dn, dqk) = dims
    n_rows = n_heads * dn
    tok = pages_per_sub * page
    pages_per_step = n_sub * pages_per_sub
    n_total = n_req * n_steps
    pad_rows = wext.shape[0] - n_rows

    def copies(g, slot):
        b = g // n_steps
        c = g % n_steps
        out = []
        for p in range(pages_per_step):
            pid = pt_ref[b, c * pages_per_step + p]
            sub, pp = divmod(p, pages_per_sub)
            out.append(pltpu.make_async_copy(cache_ckv.at[layer, pid],
                                             cbuf.at[slot, sub, pl.ds(pp * page, page)], sems.at[0, slot, sub]))
            out.append(pltpu.make_async_copy(cache_krt.at[layer, pid],
                                             rbuf.at[slot, sub, :, pl.ds(pp * page, page)], sems.at[1, slot, sub]))
        return out

    table_copies = [pltpu.make_async_copy(cc_hbm, cc_ref, tbl_sems.at[0]),
                    pltpu.make_async_copy(ss_hbm, ss_ref, tbl_sems.at[1])]
    for cp in table_copies:
        cp.start()
    wext[0:n_rows, :] = wukt_ref[...]
    m_s[...] = jnp.zeros(m_s.shape, F32)
    l_s[...] = jnp.zeros(l_s.shape, F32)
    acc_s[...] = jnp.zeros(acc_s.shape, F32)
    for g0 in range(DMA_LOOKAHEAD):
        for cp in copies(g0, g0):
            cp.start()
    for cp in table_copies:
        cp.wait()

    def body(g, carry):
        slot = g % DMA_SLOTS
        b = g // n_steps
        c = g % n_steps
        first = c == 0

        for cp in copies(g, slot):
            cp.wait()

        def rows16(x):
            return jnp.concatenate([x, jnp.zeros((pad_rows - n_heads, x.shape[1]), F32)], axis=0).astype(BF16)

        wext[n_rows:n_rows + pad_rows, :] = rows16(qt_ref[b])
        w = wext[...]
        qa = rows16(qa_ref[b])
        qb = rows16(qb_ref[b])
        def rotary(sub):
            krt = rbuf[slot, sub]
            tbl = c * n_sub + sub
            sa = jnp.dot(qa, (krt * cc_ref[tbl]).astype(BF16), preferred_element_type=F32)[0:n_heads]
            sb = jnp.dot(qb, (krt * ss_ref[tbl]).astype(BF16), preferred_element_type=F32)[0:n_heads]
            return sa + sb, jnp.sum(krt * krt, axis=0, keepdims=True)

        def weigh(cb, big, sr, krsq):
            kt = big[0:n_rows].reshape(n_heads, dn, tok)
            ssq = jnp.sum(kt * kt, axis=1)
            sn = big[n_rows:n_rows + n_heads]
            s = (sn + sr) * lax.rsqrt((ssq + krsq) / dqk + EPS)
            mc = jnp.max(s, axis=-1, keepdims=True)
            p = jnp.exp(s - mc)
            lc = jnp.sum(p, axis=-1, keepdims=True)
            p16 = jnp.concatenate([p, jnp.zeros((pad_rows - n_heads, tok), F32)], axis=0).astype(BF16)
            return mc, lc, jnp.dot(p16, cb, preferred_element_type=F32)[0:n_heads]

        subs = range(n_sub)
        rot = [rotary(sub) for sub in subs]
        cbs = [cbuf[slot, sub].astype(BF16) for sub in subs]
        bigs = [_nt_dot(w, cbs[sub]) for sub in subs]

        for cp in copies(jnp.minimum(g + DMA_LOOKAHEAD, n_total - 1), (g + DMA_LOOKAHEAD) % DMA_SLOTS):
            cp.start()

        parts = [weigh(cbs[sub], bigs[sub], *rot[sub]) for sub in subs]

        m_run = jnp.where(first, sself_ref[b], m_s[...])
        l_run = jnp.where(first, 1.0, l_s[...])
        acc = jnp.where(first, jnp.broadcast_to(cnew_ref[b], acc_s.shape), acc_s[...])
        for mc, lc, pv in parts:
            m_new = jnp.maximum(m_run, mc)
            wa = jnp.exp(m_run - m_new)
            wb = jnp.exp(mc - m_new)
            l_run = l_run * wa + lc * wb
            acc = acc * wa + pv * wb
            m_run = m_new
        m_s[...], l_s[...], acc_s[...] = m_run, l_run, acc
        lat_ref[b] = acc * (1.0 / l_run)
        return carry

    lax.fori_loop(0, trips_ref[0], body, 0)
    for g1 in range(n_total, n_total + DMA_LOOKAHEAD):
        for cp in copies(n_total - 1, g1 % DMA_SLOTS):
            cp.wait()


def _attn_sample(page_table, wukt, qt, qa, qb, sself, cnew, cc3, ss3, cache_ckv, cache_krt,
                 layer, n_sub, pages_per_sub, n_heads, dn, dqk):
    n_req, n_pages = page_table.shape
    page = cache_ckv.shape[2]
    kv_rank = cache_ckv.shape[3]
    rope = cache_krt.shape[2]
    tok = pages_per_sub * page
    pages_per_step = n_sub * pages_per_sub
    n_steps = n_pages // pages_per_step
    assert n_req * n_steps >= DMA_LOOKAHEAD
    pad_rows = 2 * SUBLANES * pl.cdiv(n_heads, 2 * SUBLANES)
    dims = (layer, n_req, n_steps, n_sub, pages_per_sub, page, n_heads, dn, dqk)
    full = lambda *shape: _resident(shape, lambda i, pt, trips: (0,) * len(shape))
    grid_spec = pltpu.PrefetchScalarGridSpec(
        num_scalar_prefetch=2,
        grid=(1,),
        in_specs=[
            full(n_heads * dn, kv_rank),
            full(n_req, n_heads, kv_rank),
            full(n_req, n_heads, rope),
            full(n_req, n_heads, rope),
            full(n_req, n_heads, 1),
            full(n_req, 1, kv_rank),
            pl.BlockSpec(memory_space=pl.ANY),
            pl.BlockSpec(memory_space=pl.ANY),
            pl.BlockSpec(memory_space=pl.ANY),
            pl.BlockSpec(memory_space=pl.ANY),
        ],
        out_specs=pl.BlockSpec((n_req, n_heads, kv_rank), lambda i, pt, trips: (0, 0, 0)),
        scratch_shapes=[
            pltpu.VMEM((n_heads * dn + pad_rows, kv_rank), BF16),
            pltpu.VMEM((DMA_SLOTS, n_sub, tok, kv_rank), F32),
            pltpu.VMEM((DMA_SLOTS, n_sub, rope, tok), F32),
            pltpu.SemaphoreType.DMA((2, DMA_SLOTS, n_sub)),
            pltpu.VMEM((n_heads, 1), F32),
            pltpu.VMEM((n_heads, 1), F32),
            pltpu.VMEM((n_heads, kv_rank), F32),
            pltpu.VMEM(cc3.shape, F32),
            pltpu.VMEM(ss3.shape, F32),
            pltpu.SemaphoreType.DMA((2,)),
        ],
    )
    return pl.pallas_call(
        functools.partial(_attn_sample_kernel, dims=dims),
        grid_spec=grid_spec,
        out_shape=jax.ShapeDtypeStruct((n_req, n_heads, kv_rank), F32),
        compiler_params=_params(("arbitrary",)),
        name="attn_sample",
    )(page_table, jnp.full((1,), n_req * n_steps, jnp.int32), wukt, qt, qa, qb, sself, cnew, cc3, ss3,
      cache_ckv, cache_krt)


def _outproj_kernel(mix_ref, ya_ref, x_ref, g1_ref, sh2_ref, sc2_ref, gffn_ref, wo_ref, x1_ref, h2_ref,
                    *, mix_dim):
    y = jnp.dot(mix_ref[...], wo_ref[0:mix_dim, :], preferred_element_type=F32)
    y = y + jnp.dot(ya_ref[...], wo_ref[mix_dim:, :], preferred_element_type=F32)
    x1 = x_ref[...] + g1_ref[...] * y
    x1_ref[...] = x1
    h2_ref[...] = (_rms(x1, gffn_ref[...]) * (1.0 + sc2_ref[...]) + sh2_ref[...]).astype(BF16)


def _outproj_sample_kernel(mix_ref, lat_ref, wuv_ref, x_ref, g1_ref, sh2_ref, sc2_ref, gffn_ref, wo_ref,
                           x1_ref, h2_ref, *, mix_dim, n_heads, dv):
    y = jnp.dot(mix_ref[...], wo_ref[0:mix_dim, :], preferred_element_type=F32)
    for h in range(n_heads):
        ya = jnp.dot(lat_ref[h].astype(BF16), wuv_ref[h], preferred_element_type=F32).astype(BF16)
        y = y + jnp.dot(ya, wo_ref[mix_dim + h * dv:mix_dim + (h + 1) * dv, :], preferred_element_type=F32)
    x1 = x_ref[...] + g1_ref[...] * y
    x1_ref[...] = x1
    h2_ref[...] = (_rms(x1, gffn_ref[...]) * (1.0 + sc2_ref[...]) + sh2_ref[...]).astype(BF16)


def _outproj(mix, ya, x, mod, g_ffn, w_o, layer, tm, rows_per_batch):
    m, d_model = x.shape
    mix_dim = mix.shape[-1]
    att_dim = ya.shape[-1]
    return pl.pallas_call(
        functools.partial(_outproj_kernel, mix_dim=mix_dim),
        grid=(m // tm,),
        in_specs=[
            pl.BlockSpec((tm, mix_dim), lambda i: (i, 0)),
            pl.BlockSpec((tm, att_dim), lambda i: (i, 0)),
            pl.BlockSpec((tm, d_model), lambda i: (i, 0)),
            _mod_spec(mod, layer, 2, tm, rows_per_batch),
            _mod_spec(mod, layer, 3, tm, rows_per_batch),
            _mod_spec(mod, layer, 4, tm, rows_per_batch),
            pl.BlockSpec((None, 1, d_model), lambda i: (layer, 0, 0)),
            _resident((None, mix_dim + att_dim, d_model), lambda i: (layer, 0, 0)),
        ],
        out_specs=[
            pl.BlockSpec((tm, d_model), lambda i: (i, 0)),
            pl.BlockSpec((tm, d_model), lambda i: (i, 0)),
        ],
        out_shape=[
            jax.ShapeDtypeStruct((m, d_model), F32),
            jax.ShapeDtypeStruct((m, d_model), BF16),
        ],
        compiler_params=_params(("arbitrary",)),
        name="outproj",
    )(mix, ya, x, mod, mod, mod, g_ffn, w_o)


def _outproj_sample(mix, lat_t, w_uv, x, mod, g_ffn, w_o, layer):
    m, d_model = x.shape
    mix_dim = mix.shape[-1]
    n_heads, _, kv_rank = lat_t.shape
    dv = w_uv.shape[-1]
    return pl.pallas_call(
        functools.partial(_outproj_sample_kernel, mix_dim=mix_dim, n_heads=n_heads, dv=dv),
        grid=(1,),
        in_specs=[
            pl.BlockSpec((m, mix_dim), lambda i: (0, 0)),
            pl.BlockSpec((n_heads, m, kv_rank), lambda i: (0, 0, 0)),
            pl.BlockSpec((None, n_heads, kv_rank, dv), lambda i: (layer, 0, 0, 0)),
            pl.BlockSpec((m, d_model), lambda i: (0, 0)),
            _mod_spec(mod, layer, 2, m, 1),
            _mod_spec(mod, layer, 3, m, 1),
            _mod_spec(mod, layer, 4, m, 1),
            pl.BlockSpec((None, 1, d_model), lambda i: (layer, 0, 0)),
            _resident((None, mix_dim + n_heads * dv, d_model), lambda i: (layer, 0, 0)),
        ],
        out_specs=[
            pl.BlockSpec((m, d_model), lambda i: (0, 0)),
            pl.BlockSpec((m, d_model), lambda i: (0, 0)),
        ],
        out_shape=[
            jax.ShapeDtypeStruct((m, d_model), F32),
            jax.ShapeDtypeStruct((m, d_model), BF16),
        ],
        compiler_params=_params(("arbitrary",)),
        name="outproj_sample",
    )(mix, lat_t, w_uv, x, mod, mod, mod, g_ffn, w_o)


def _ffn_kernel(h_ref, x1_hbm, g2_ref, w1_ref, w3_ref, w2_ref, o_ref, x1_buf, x1_sem, *, n_split):
    i = pl.program_id(0)
    k = pl.program_id(1)
    tm = x1_buf.shape[0]
    x1_copy = pltpu.make_async_copy(x1_hbm.at[pl.ds(pl.multiple_of(i * tm, tm), tm)], x1_buf, x1_sem)

    @pl.when(k == 0)
    def _():
        x1_copy.start()
        o_ref[...] = jnp.zeros(o_ref.shape, F32)

    h = h_ref[...]
    hf = w1_ref.shape[-1] // n_split
    ab = [(jnp.dot(h, w1_ref[:, t * hf:(t + 1) * hf], preferred_element_type=F32),
           jnp.dot(h, w3_ref[:, t * hf:(t + 1) * hf], preferred_element_type=F32)) for t in range(n_split)]
    for t, (a, b) in enumerate(ab):
        f = (a * jax.nn.sigmoid(a) * b).astype(BF16)
        o_ref[...] += jnp.dot(f, w2_ref[t * hf:(t + 1) * hf, :], preferred_element_type=F32)

    @pl.when(k == pl.num_programs(1) - 1)
    def _():
        x1_copy.wait()
        o_ref[...] = x1_buf[...] + g2_ref[...] * o_ref[...]


def _ffn(h2, x1, mod, w1, w3, w2, layer, tm, fc, rows_per_batch):
    m, d_model = x1.shape
    d_ff = w1.shape[-1]
    mod_spec = _mod_spec(mod, layer, 5, tm, rows_per_batch)
    mod_map = mod_spec.index_map
    return pl.pallas_call(
        functools.partial(_ffn_kernel, n_split=FFN_SPLIT if fc % (FFN_SPLIT * LANES) == 0 else 1),
        grid=(m // tm, d_ff // fc),
        in_specs=[
            pl.BlockSpec((tm, d_model), lambda i, k: (i, 0)),
            pl.BlockSpec(memory_space=pl.ANY),
            pl.BlockSpec(mod_spec.block_shape, lambda i, k: mod_map(i)),
            pl.BlockSpec((None, d_model, fc), lambda i, k: (layer, 0, k)),
            pl.BlockSpec((None, d_model, fc), lambda i, k: (layer, 0, k)),
            pl.BlockSpec((None, fc, d_model), lambda i, k: (layer, k, 0)),
        ],
        out_specs=pl.BlockSpec((tm, d_model), lambda i, k: (i, 0)),
        out_shape=jax.ShapeDtypeStruct((m, d_model), F32),
        scratch_shapes=[pltpu.VMEM((tm, d_model), F32), pltpu.SemaphoreType.DMA(())],
        compiler_params=_params(("arbitrary", "arbitrary")),
        name="ffn",
    )(h2, x1, mod, w1, w3, w2)


def _spread_last(x, half):
    pad = [(0, 0)] * (x.ndim - 1) + [(0, ROPE_HALF_OFF - half)]
    return jnp.concatenate([jnp.pad(x[..., :half], pad), jnp.pad(x[..., half:], pad)], axis=-1)


def _compact_last(x, half):
    return jnp.concatenate([x[..., :half], x[..., ROPE_HALF_OFF:ROPE_HALF_OFF + half]], axis=-1)


def _rope_tables(pos, half):
    inv = ROPE_THETA ** (-jnp.arange(half, dtype=F32) / half)
    ang = pos.astype(F32)[:, None] * inv[None, :]
    return jnp.cos(ang), jnp.sin(ang)


def kernel(x_prompt, x_sample, cache_ckv, cache_kr, state_conv, state_pool, page_table,
           c_prompt, c_sample, g_mix, w_ada, b_ada, w_in, w_conv, w_pool, s_pool,
           g_cq, g_ckv, w_uq, w_uk, w_uv, g_q, g_k, w_o, g_ffn, w1, w3, w2):
    n_batch, n_tok, d_model = x_prompt.shape
    n_req, dec_seq, _ = x_sample.shape
    assert dec_seq == 1, "the sample group carries one new token per request"
    n_layers = w_in.shape[0]
    conv_w, conv_dim = w_conv.shape[1:]
    pool_dim = s_pool.shape[1]
    pool_ctx = state_pool.shape[2]
    assert pool_ctx == max(POOL_WINDOWS) - 1 and w_pool.shape[1] == len(POOL_WINDOWS)
    q_rank = g_cq.shape[1]
    kv_rank = g_ckv.shape[1]
    n_heads, dqk = w_uq.shape[2:]
    dn = w_uk.shape[3]
    rope = dqk - dn
    half = rope // 2
    dv = w_uv.shape[3]
    page = cache_ckv.shape[2]
    n_pages = page_table.shape[1]
    past = n_pages * page
    base = w_in.shape[2] - rope
    assert base == 3 * conv_dim + pool_dim + q_rank + kv_rank and half <= ROPE_HALF_OFF
    assert dn % LANES == 0 and n_tok >= 2 * SUBLANES
    cfg = (conv_dim, pool_dim, q_rank, kv_rank, n_heads, dn, dqk, conv_w, pool_ctx)

    w_in_b = w_in.astype(BF16)
    w_kr = _spread_last(w_in[..., base:], half).astype(BF16)
    w_uq_p = jnp.concatenate([w_uq[..., :dn], _spread_last(w_uq[..., dn:], half)], axis=-1)
    w_uq_p = w_uq_p.reshape(n_layers, q_rank, n_heads * (dn + ROPE_SPREAD)).astype(BF16)
    g_q_p = jnp.concatenate([g_q[:, :dn], _spread_last(g_q[:, dn:], half)], axis=-1)[:, None, :]
    g_kr = _spread_last(g_k[:, dn:], half)[:, None, :]
    g_krb = _spread_last(jnp.concatenate([g_k[:, dn:dn + half], -g_k[:, dn + half:]], axis=-1), half)[:, None, :]
    w_uk_b = w_uk.reshape(n_layers, kv_rank, n_heads * dn).astype(BF16)
    lw = {
        "w_conv": w_conv, "w_pool": w_pool.astype(BF16), "s_pool": s_pool[:, None, :],
        "g_cq": g_cq[:, None, :], "g_ckv": g_ckv[:, None, :], "w_uq": w_uq_p, "w_uk": w_uk_b,
        "g_q": g_q_p, "g_kn": g_k[:, None, :dn], "g_kr": g_kr, "g_krb": g_krb,
    }
    w_ukt = jnp.swapaxes(w_uk_b, 1, 2)
    w_uv_b = jnp.transpose(w_uv, (0, 2, 1, 3)).astype(BF16)
    w_o_b = w_o.astype(BF16)
    w1_b, w3_b, w2_b = w1.astype(BF16), w3.astype(BF16), w2.astype(BF16)
    g_mix3, g_ffn3 = g_mix[:, None, :], g_ffn[:, None, :]

    cos_p, sin_p = _rope_tables(jnp.arange(n_tok), half)
    cc_p = _spread_last(jnp.concatenate([cos_p, cos_p], axis=-1), half)
    ss_p = _spread_last(jnp.concatenate([-sin_p, sin_p], axis=-1), half)
    cos_s, sin_s = _rope_tables(past + jnp.arange(dec_seq), half)
    cc_s = _spread_last(jnp.concatenate([cos_s, cos_s], axis=-1), half)
    ss_s = _spread_last(jnp.concatenate([-sin_s, sin_s], axis=-1), half)
    tm_p = math.gcd(n_tok, 512)
    tm_ffn = math.gcd(n_tok, 1024)
    tc_p = math.gcd(n_tok, 512)
    tq_p = math.gcd(n_tok, 256)
    tk_p = 2 * tq_p if n_tok % (2 * tq_p) == 0 else tq_p
    fc = math.gcd(w1.shape[2], 512)
    pages_per_sub = math.gcd(n_pages, 4)
    n_sub = math.gcd(n_pages // pages_per_sub, 8)

    cos_c, sin_c = _rope_tables(jnp.arange(past), half)
    sub_tok = pages_per_sub * page
    by_sub = lambda t: jnp.transpose(t.reshape(past // sub_tok, sub_tok, rope), (0, 2, 1))
    cc_c = by_sub(jnp.concatenate([cos_c, cos_c], axis=-1))
    ss_c = by_sub(jnp.concatenate([sin_c, sin_c], axis=-1))
    cache_krt = jnp.swapaxes(cache_kr, 2, 3)

    mod = _ada_mod(jnp.concatenate([c_prompt, c_sample], axis=0), w_ada, b_ada, math.gcd(d_model, 1024))
    mod_p = mod[:, :, :n_batch, None, :]
    mod_s = mod[:, :, n_batch:, :]

    st_conv = jnp.swapaxes(state_conv, 1, 2)
    st_pool = jnp.swapaxes(state_pool, 1, 2)

    xp = x_prompt.reshape(n_batch * n_tok, d_model)
    xs = x_sample.reshape(n_req, d_model)
    outs = {k: [] for k in ("ckv_p", "kr_p", "conv_p", "pool_p", "ckv_s", "kr_s", "conv_s", "pool_s")}
    for l in range(n_layers):
        z = _inproj(xp, mod_p, g_mix3, w_in_b, w_kr, base, l, tm_p, n_tok)
        mix, q, k, ckvn, vb, kr_sp, cst, pst = _mix_prompt(z, cc_p, ss_p, lw, l, n_batch, n_tok, tc_p, cfg)
        ya = _attn_prompt(q, k, vb, w_uv_b, l, tq_p, tk_p)
        x1, h2 = _outproj(mix, ya, xp, mod_p, g_ffn3, w_o_b, l, tm_p, n_tok)
        xp = _ffn(h2, x1, mod_p, w1_b, w3_b, w2_b, l, tm_ffn, fc, n_tok)
        outs["ckv_p"].append(ckvn)
        outs["kr_p"].append(_compact_last(kr_sp, half))
        outs["conv_p"].append(cst)
        outs["pool_p"].append(pst)

        z = _inproj(xs, mod_s, g_mix3, w_in_b, w_kr, base, l, n_req, 1)
        mix, u_new, ckv_new, qt, qa, qb, sself = _mix_sample(z, st_conv, st_pool, cc_s, ss_s, lw, l, cfg, past)
        by_req = lambda a: jnp.swapaxes(a, 0, 1)
        lat = _attn_sample(page_table, w_ukt[l], by_req(qt), by_req(_compact_last(qa, half)),
                           by_req(_compact_last(qb, half)), sself[:, :, None], ckv_new[:, None, :],
                           cc_c, ss_c, cache_ckv, cache_krt, l, n_sub, pages_per_sub, n_heads, dn, dqk)
        x1, h2 = _outproj_sample(mix, jnp.swapaxes(lat, 0, 1), w_uv_b, xs, mod_s, g_ffn3, w_o_b, l)
        xs = _ffn(h2, x1, mod_s, w1_b, w3_b, w2_b, l, n_req, fc, 1)
        outs["ckv_s"].append(ckv_new[:, None, :])
        outs["kr_s"].append(_compact_last(z[:, base:], half)[:, None, :])
        outs["conv_s"].append(jnp.concatenate([state_conv[l][:, 1:], u_new[:, None, :]], axis=1))
        outs["pool_s"].append(jnp.concatenate([state_pool[l][:, 1:], z[:, None, 3 * conv_dim:3 * conv_dim + pool_dim]], axis=1))

    st = {k: jnp.stack(v) for k, v in outs.items()}
    return (xp.reshape(n_batch, n_tok, d_model), xs.reshape(n_req, dec_seq, d_model),
            st["ckv_p"], st["kr_p"], st["conv_p"], st["pool_p"],
            st["ckv_s"], st["kr_s"], st["conv_s"], st["pool_s"])
```

```python
import functools
import math

import jax
import jax.numpy as jnp
from jax import lax
from jax.experimental import pallas as pl
from jax.experimental.pallas import tpu as pltpu

F32 = jnp.float32
BF16 = jnp.bfloat16

EPS = 1e-6
ROPE_THETA = 10000.0
POOL_WINDOWS = (2, 4, 8, 16)

LANES = 128
SUBLANES = 8
VMEM_BYTES_V7X = 64 * 1024 * 1024
VMEM_LIMIT = VMEM_BYTES_V7X - 8 * 1024 * 1024

ROPE_SPREAD = LANES
ROPE_HALF_OFF = LANES // 2
HEAD_SKEW = 1
FFN_SPLIT = 2
DMA_LOOKAHEAD = 2
DMA_SLOTS = DMA_LOOKAHEAD + 1


def _params(semantics):
    return pltpu.CompilerParams(dimension_semantics=semantics, vmem_limit_bytes=VMEM_LIMIT)


def _resident(block_shape, index_map):
    return pl.BlockSpec(block_shape, index_map, pipeline_mode=pl.Buffered(1))


def _rms(x, g):
    return x * lax.rsqrt(jnp.mean(x * x, axis=-1, keepdims=True) + EPS) * g


def _nt_dot(a, b):
    return lax.dot_general(a, b, (((1,), (1,)), ((), ())), preferred_element_type=F32)


def _rope_spread(x, cc, ss):
    return x * cc + pltpu.roll(x, ROPE_HALF_OFF, 1) * ss


def _ada_kernel(c_ref, w_ref, b_ref, o_ref):
    c = c_ref[...]
    a = (c * jax.nn.sigmoid(c)).astype(BF16)
    o_ref[...] = jnp.dot(a, w_ref[...].astype(BF16), preferred_element_type=F32) + b_ref[...]


def _ada_mod(c_all, w_ada, b_ada, tn):
    n_layers, d_model, six_d = w_ada.shape
    rows = c_all.shape[0]
    n_per = d_model // tn
    return pl.pallas_call(
        _ada_kernel,
        grid=(n_layers, 6, n_per),
        in_specs=[
            pl.BlockSpec((rows, d_model), lambda l, k, n: (0, 0)),
            pl.BlockSpec((None, d_model, tn), lambda l, k, n: (l, 0, k * n_per + n)),
            pl.BlockSpec((None, 1, tn), lambda l, k, n: (l, 0, k * n_per + n)),
        ],
        out_specs=pl.BlockSpec((None, None, rows, tn), lambda l, k, n: (l, k, 0, n)),
        out_shape=jax.ShapeDtypeStruct((n_layers, 6, rows, d_model), F32),
        compiler_params=_params(("arbitrary", "arbitrary", "arbitrary")),
        name="ada_mod",
    )(c_all, w_ada, b_ada.reshape(n_layers, 1, six_d))


def _mod_spec(mod, layer, chunk, tm, rows_per_batch):
    if mod.ndim == 5:
        tiles_per_batch = rows_per_batch // tm
        return pl.BlockSpec((None, None, None, 1, mod.shape[-1]),
                            lambda i: (layer, chunk, i // tiles_per_batch, 0, 0))
    return pl.BlockSpec((None, None, tm, mod.shape[-1]), lambda i: (layer, chunk, i, 0))


def _inproj_kernel(x_ref, sh_ref, sc_ref, g_ref, w_ref, wkr_ref, z_ref):
    h = (_rms(x_ref[...], g_ref[...]) * (1.0 + sc_ref[...]) + sh_ref[...]).astype(BF16)
    n_main = w_ref.shape[-1]
    z_ref[:, 0:n_main] = jnp.dot(h, w_ref[...], preferred_element_type=F32)
    z_ref[:, n_main:] = jnp.dot(h, wkr_ref[...], preferred_element_type=F32)


def _inproj(x, mod, g_mix, w_in_b, w_kr, n_main, layer, tm, rows_per_batch):
    m, d_model = x.shape
    assert n_main % LANES == 0
    n_out = n_main + w_kr.shape[-1]
    return pl.pallas_call(
        _inproj_kernel,
        grid=(m // tm,),
        in_specs=[
            pl.BlockSpec((tm, d_model), lambda i: (i, 0)),
            _mod_spec(mod, layer, 0, tm, rows_per_batch),
            _mod_spec(mod, layer, 1, tm, rows_per_batch),
            pl.BlockSpec((None, 1, d_model), lambda i: (layer, 0, 0)),
            _resident((None, d_model, n_main), lambda i: (layer, 0, 0)),
            _resident((None, d_model, w_kr.shape[-1]), lambda i: (layer, 0, 0)),
        ],
        out_specs=pl.BlockSpec((tm, n_out), lambda i: (i, 0)),
        out_shape=jax.ShapeDtypeStruct((m, n_out), F32),
        compiler_params=_params(("arbitrary",)),
        name="inproj",
    )(x, mod, mod, g_mix, w_in_b, w_kr)


def _queries(cq, gcq, wuq, gq, cc, ss, n_heads, dn, dqk, scale):
    cqn = _rms(cq, gcq).astype(BF16)
    qall = jnp.dot(cqn, wuq, preferred_element_type=F32)
    hw = dn + ROPE_SPREAD
    out = []
    for h in range(n_heads):
        qn = qall[:, h * hw:h * hw + dn]
        qr = qall[:, h * hw + dn:(h + 1) * hw]
        ssq = jnp.sum(qn * qn, axis=-1, keepdims=True) + jnp.sum(qr * qr, axis=-1, keepdims=True)
        r = lax.rsqrt(ssq / dqk + EPS) * scale
        qn = qn * r * gq[:, :dn]
        qr = _rope_spread(qr * r * gq[:, dn:], cc, ss)
        out.append((qn, qr))
    return out


def _keys(ckvn_b, kr, wuk, gkn, gkr, cc, ss, n_heads, dn, dqk):
    kn_all = jnp.dot(ckvn_b, wuk, preferred_element_type=F32)
    krsq = jnp.sum(kr * kr, axis=-1, keepdims=True)
    krot = _rope_spread(kr * gkr, cc, ss)
    out = []
    for h in range(n_heads):
        kn = kn_all[:, h * dn:(h + 1) * dn]
        r = lax.rsqrt((jnp.sum(kn * kn, axis=-1, keepdims=True) + krsq) / dqk + EPS)
        out.append((kn * r * gkn, krot * r))
    return out


def _mix_prompt_kernel(z_ref, cc_ref, ss_ref, wconv_ref, wpool_ref, spool_ref, gcq_ref, gckv_ref,
                       wuq_ref, wuk_ref, gq_ref, gkn_ref, gkr_ref,
                       mix_ref, q_ref, k_ref, ckv_ref, vb_ref, kr_ref, cst_ref, pst_ref,
                       ubuf, vbuf, *, dims):
    (tc, conv_dim, pool_dim, q_rank, kv_rank, n_heads, dn, dqk, scale, conv_w, pool_ctx) = dims
    j = pl.program_id(1)
    halo_u = SUBLANES
    halo_v = 2 * SUBLANES

    @pl.when(j == 0)
    def _():
        ubuf[0:halo_u, :] = jnp.zeros((halo_u, conv_dim), F32)
        vbuf[0:halo_v, :] = jnp.zeros((halo_v, pool_dim), F32)

    o = 0
    hc = z_ref[:, o:o + conv_dim]; o += conv_dim
    bc = z_ref[:, o:o + conv_dim]; o += conv_dim
    cg = z_ref[:, o:o + conv_dim]; o += conv_dim
    vp = z_ref[:, o:o + pool_dim]; o += pool_dim
    cq = z_ref[:, o:o + q_rank]; o += q_rank
    ckv = z_ref[:, o:o + kv_rank]; o += kv_rank
    kr = z_ref[:, o:o + ROPE_SPREAD]

    u = cg * hc
    ubuf[halo_u:halo_u + tc, :] = u
    acc = wconv_ref[conv_w - 1:conv_w, :] * u
    for t in range(1, conv_w):
        acc = acc + wconv_ref[conv_w - 1 - t:conv_w - t, :] * ubuf[pl.ds(halo_u - t, tc), :]
    mix_ref[:, 0:conv_dim] = (bc * acc).astype(BF16)
    cst_ref[...] = ubuf[pl.ds(halo_u + tc - (conv_w - 1), conv_w - 1), :]
    ubuf[0:halo_u, :] = ubuf[tc:tc + halo_u, :]

    vbuf[halo_v:halo_v + tc, :] = vp
    pg = pool_dim // len(POOL_WINDOWS)
    pos = j * tc + lax.broadcasted_iota(jnp.int32, (tc, pg), 0)
    for g, w in enumerate(POOL_WINDOWS):
        sl = slice(g * pg, (g + 1) * pg)
        v = vp[:, sl]
        s = v
        for t in range(1, w):
            s = s + vbuf[pl.ds(halo_v - t, tc), sl]
        cnt = jnp.minimum(pos + 1, w).astype(F32)
        d = s / cnt - v
        y = jnp.dot(d.astype(BF16), wpool_ref[g], preferred_element_type=F32) * spool_ref[:, sl]
        mix_ref[:, conv_dim + g * pg:conv_dim + (g + 1) * pg] = y.astype(BF16)
    pst_ref[...] = vbuf[pl.ds(halo_v + tc - pool_ctx, pool_ctx), :]
    vbuf[0:halo_v, :] = vbuf[tc:tc + halo_v, :]

    cc = cc_ref[...]
    ss = ss_ref[...]
    qs = _queries(cq, gcq_ref[...], wuq_ref[...], gq_ref[...], cc, ss, n_heads, dn, dqk, scale)
    for h, (qn, qr) in enumerate(qs):
        q_ref[h, :, 0:dn] = qn.astype(BF16)
        q_ref[h, :, dn:dn + ROPE_SPREAD] = qr.astype(BF16)
    ckvn = _rms(ckv, gckv_ref[...])
    ckv_ref[...] = ckvn
    ckvn_b = ckvn.astype(BF16)
    vb_ref[...] = ckvn_b
    kr_ref[...] = kr
    ks = _keys(ckvn_b, kr, wuk_ref[...], gkn_ref[...], gkr_ref[...], cc, ss, n_heads, dn, dqk)
    for h, (kn, krot) in enumerate(ks):
        k_ref[h, :, 0:dn] = kn.astype(BF16)
        k_ref[h, :, dn:dn + ROPE_SPREAD] = krot.astype(BF16)


def _mix_prompt(z, cc, ss, lw, layer, n_batch, n_tok, tc, cfg):
    (conv_dim, pool_dim, q_rank, kv_rank, n_heads, dn, dqk, conv_w, pool_ctx) = cfg
    n_in = z.shape[-1]
    hw = dn + ROPE_SPREAD
    tiles = n_tok // tc
    dims = (tc, conv_dim, pool_dim, q_rank, kv_rank, n_heads, dn, dqk, dqk ** -0.5 * math.log2(math.e),
            conv_w, pool_ctx)
    lay = lambda *shape: pl.BlockSpec((None,) + shape, lambda b, j: (layer,) + (0,) * len(shape))
    n_groups = len(POOL_WINDOWS)
    pg = pool_dim // n_groups
    mix_dim = conv_dim + pool_dim
    return pl.pallas_call(
        functools.partial(_mix_prompt_kernel, dims=dims),
        grid=(n_batch, tiles),
        in_specs=[
            pl.BlockSpec((tc, n_in), lambda b, j: (b * tiles + j, 0)),
            pl.BlockSpec((tc, ROPE_SPREAD), lambda b, j: (j, 0)),
            pl.BlockSpec((tc, ROPE_SPREAD), lambda b, j: (j, 0)),
            lay(conv_w, conv_dim),
            lay(n_groups, pg, pg),
            lay(1, pool_dim),
            lay(1, q_rank),
            lay(1, kv_rank),
            lay(q_rank, n_heads * hw),
            lay(kv_rank, n_heads * dn),
            lay(1, hw),
            lay(1, dn),
            lay(1, ROPE_SPREAD),
        ],
        out_specs=[
            pl.BlockSpec((tc, mix_dim), lambda b, j: (b * tiles + j, 0)),
            pl.BlockSpec((None, n_heads, tc, hw), lambda b, j: (b, 0, j, 0)),
            pl.BlockSpec((None, n_heads, tc, hw), lambda b, j: (b, 0, j, 0)),
            pl.BlockSpec((None, tc, kv_rank), lambda b, j: (b, j, 0)),
            pl.BlockSpec((None, tc, kv_rank), lambda b, j: (b, j, 0)),
            pl.BlockSpec((None, tc, ROPE_SPREAD), lambda b, j: (b, j, 0)),
            pl.BlockSpec((None, conv_w - 1, conv_dim), lambda b, j: (b, 0, 0)),
            pl.BlockSpec((None, pool_ctx, pool_dim), lambda b, j: (b, 0, 0)),
        ],
        out_shape=[
            jax.ShapeDtypeStruct((n_batch * n_tok, mix_dim), BF16),
            jax.ShapeDtypeStruct((n_batch, n_heads, n_tok, hw), BF16),
            jax.ShapeDtypeStruct((n_batch, n_heads, n_tok, hw), BF16),
            jax.ShapeDtypeStruct((n_batch, n_tok, kv_rank), F32),
            jax.ShapeDtypeStruct((n_batch, n_tok, kv_rank), BF16),
            jax.ShapeDtypeStruct((n_batch, n_tok, ROPE_SPREAD), F32),
            jax.ShapeDtypeStruct((n_batch, conv_w - 1, conv_dim), F32),
            jax.ShapeDtypeStruct((n_batch, pool_ctx, pool_dim), F32),
        ],
        scratch_shapes=[
            pltpu.VMEM((SUBLANES + tc, conv_dim), F32),
            pltpu.VMEM((2 * SUBLANES + tc, pool_dim), F32),
        ],
        compiler_params=_params(("arbitrary", "arbitrary")),
        name="mix_prompt",
    )(z, cc, ss, lw["w_conv"], lw["w_pool"], lw["s_pool"], lw["g_cq"], lw["g_ckv"],
      lw["w_uq"], lw["w_uk"], lw["g_q"], lw["g_kn"], lw["g_kr"])


def _mix_sample_kernel(z_ref, cst_ref, pst_ref, cc_ref, ss_ref, wconv_ref, wpool_ref, spool_ref,
                       gcq_ref, gckv_ref, wuq_ref, wuk_ref, gq_ref, gkn_ref, gkr_ref, gkrb_ref,
                       mix_ref, u_ref, ckv_ref, qt_ref, qa_ref, qb_ref, sself_ref, *, dims):
    (conv_dim, pool_dim, q_rank, kv_rank, n_heads, dn, dqk, scale, conv_w, pool_ctx, past) = dims
    o = 0
    hc = z_ref[:, o:o + conv_dim]; o += conv_dim
    bc = z_ref[:, o:o + conv_dim]; o += conv_dim
    cg = z_ref[:, o:o + conv_dim]; o += conv_dim
    vp = z_ref[:, o:o + pool_dim]; o += pool_dim
    cq = z_ref[:, o:o + q_rank]; o += q_rank
    ckv = z_ref[:, o:o + kv_rank]; o += kv_rank
    kr = z_ref[:, o:o + ROPE_SPREAD]

    u = cg * hc
    u_ref[...] = u
    acc = wconv_ref[conv_w - 1:conv_w, :] * u
    for t in range(1, conv_w):
        acc = acc + wconv_ref[conv_w - 1 - t:conv_w - t, :] * cst_ref[conv_w - 1 - t]
    mix_ref[:, 0:conv_dim] = (bc * acc).astype(BF16)

    pg = pool_dim // len(POOL_WINDOWS)
    for g, w in enumerate(POOL_WINDOWS):
        sl = slice(g * pg, (g + 1) * pg)
        v = vp[:, sl]
        s = v
        for t in range(1, w):
            s = s + pst_ref[pool_ctx - t, :, sl]
        d = s / float(min(past + 1, w)) - v
        y = jnp.dot(d.astype(BF16), wpool_ref[g], preferred_element_type=F32) * spool_ref[:, sl]
        mix_ref[:, conv_dim + g * pg:conv_dim + (g + 1) * pg] = y.astype(BF16)

    cc = cc_ref[...]
    ss = ss_ref[...]
    qs = _queries(cq, gcq_ref[...], wuq_ref[...], gq_ref[...], cc, ss, n_heads, dn, dqk, scale)
    ckvn = _rms(ckv, gckv_ref[...])
    ckv_ref[...] = ckvn
    wuk = wuk_ref[...]
    gkn = gkn_ref[...]
    ks = _keys(ckvn.astype(BF16), kr, wuk, gkn, gkr_ref[...], cc, ss, n_heads, dn, dqk)
    for h in range(n_heads):
        qn, qr = qs[h]
        kn, krot = ks[h]
        sself_ref[:, h:h + 1] = (jnp.sum(qn * kn, axis=-1, keepdims=True)
                                 + jnp.sum(qr * krot, axis=-1, keepdims=True))
        qt_ref[h] = _nt_dot((qn * gkn).astype(BF16), wuk[:, h * dn:(h + 1) * dn])
        qa_ref[h] = qr * gkr_ref[...]
        qb_ref[h] = pltpu.roll(qr, ROPE_HALF_OFF, 1) * gkrb_ref[...]


def _mix_sample(z, cst, pst, cc, ss, lw, layer, cfg, past):
    (conv_dim, pool_dim, q_rank, kv_rank, n_heads, dn, dqk, conv_w, pool_ctx) = cfg
    rows, n_in = z.shape
    hw = dn + ROPE_SPREAD
    dims = (conv_dim, pool_dim, q_rank, kv_rank, n_heads, dn, dqk, dqk ** -0.5, conv_w, pool_ctx, past)
    lay = lambda *shape: pl.BlockSpec((None,) + shape, lambda i: (layer,) + (0,) * len(shape))
    full = lambda *shape: pl.BlockSpec(shape, lambda i: (0,) * len(shape))
    n_groups = len(POOL_WINDOWS)
    pg = pool_dim // n_groups
    mix_dim = conv_dim + pool_dim
    return pl.pallas_call(
        functools.partial(_mix_sample_kernel, dims=dims),
        grid=(1,),
        in_specs=[
            full(rows, n_in),
            lay(conv_w - 1, rows, conv_dim),
            lay(pool_ctx, rows, pool_dim),
            full(1, ROPE_SPREAD),
            full(1, ROPE_SPREAD),
            lay(conv_w, conv_dim),
            lay(n_groups, pg, pg),
            lay(1, pool_dim),
            lay(1, q_rank),
            lay(1, kv_rank),
            lay(q_rank, n_heads * hw),
            lay(kv_rank, n_heads * dn),
            lay(1, hw),
            lay(1, dn),
            lay(1, ROPE_SPREAD),
            lay(1, ROPE_SPREAD),
        ],
        out_specs=[
            full(rows, mix_dim),
            full(rows, conv_dim),
            full(rows, kv_rank),
            full(n_heads, rows, kv_rank),
            full(n_heads, rows, ROPE_SPREAD),
            full(n_heads, rows, ROPE_SPREAD),
            full(rows, n_heads),
        ],
        out_shape=[
            jax.ShapeDtypeStruct((rows, mix_dim), BF16),
            jax.ShapeDtypeStruct((rows, conv_dim), F32),
            jax.ShapeDtypeStruct((rows, kv_rank), F32),
            jax.ShapeDtypeStruct((n_heads, rows, kv_rank), F32),
            jax.ShapeDtypeStruct((n_heads, rows, ROPE_SPREAD), F32),
            jax.ShapeDtypeStruct((n_heads, rows, ROPE_SPREAD), F32),
            jax.ShapeDtypeStruct((rows, n_heads), F32),
        ],
        compiler_params=_params(("arbitrary",)),
        name="mix_sample",
    )(z, cst, pst, cc, ss, lw["w_conv"], lw["w_pool"], lw["s_pool"], lw["g_cq"], lw["g_ckv"],
      lw["w_uq"], lw["w_uk"], lw["g_q"], lw["g_kn"], lw["g_kr"], lw["g_krb"])


def _attn_prompt_kernel(q_ref, k_ref, v_ref, wuv_ref, o_ref, m_s, l_s, acc_s, *, tq, tk, n_heads, dv):
    i = pl.program_id(1)
    acc_tiles = acc_s.shape[-1] // LANES
    m_s[...] = jnp.full(m_s.shape, -jnp.inf, F32)
    l_s[...] = jnp.zeros(l_s.shape, F32)
    acc_s[...] = jnp.zeros(acc_s.shape, F32)

    def step(start, width, masked):
        v = v_ref[pl.ds(start, width), :]

        def scores(h):
            s = _nt_dot(q_ref[h], k_ref[h, pl.ds(start, width), :])
            if masked:
                row = lax.broadcasted_iota(jnp.int32, s.shape, 0)
                col = lax.broadcasted_iota(jnp.int32, s.shape, 1)
                s = jnp.where(col <= row, s, -jnp.inf)
            return s

        def update(h, s):
            m_prev = m_s[h]
            m_new = jnp.maximum(m_prev, jnp.max(s, axis=-1, keepdims=True))
            corr = jnp.exp2(m_prev - m_new)
            ps = [jnp.exp2(s[:, t * LANES:(t + 1) * LANES] - m_new) for t in range(width // LANES)]
            l_s[h] = l_s[h] * corr + functools.reduce(lambda a, b: a + b, ps)
            p = jnp.concatenate(ps, axis=1).astype(BF16)
            pv = jnp.dot(p, v, preferred_element_type=F32)
            acc_s[h] = acc_s[h] * jnp.concatenate([corr] * acc_tiles, axis=1) + pv
            m_s[h] = m_new

        pending = {}
        for t in range(n_heads + HEAD_SKEW):
            if t < n_heads:
                pending[t] = scores(t)
            if t >= HEAD_SKEW:
                update(t - HEAD_SKEW, pending.pop(t - HEAD_SKEW))

    ratio = tk // tq

    def body(j, carry):
        step(pl.multiple_of(j * tk, tk), tk, False)
        return carry

    lax.fori_loop(0, i // ratio, body, 0)
    for r in range(1, ratio):
        @pl.when(i % ratio >= r)
        def _():
            step(pl.multiple_of((i // ratio) * tk + (r - 1) * tq, tq), tq, False)
    step(pl.multiple_of(i * tq, tq), tq, True)
    for h in range(n_heads):
        inv_l = 1.0 / jnp.sum(l_s[h], axis=-1, keepdims=True)
        lat = (acc_s[h] * inv_l).astype(BF16)
        o_ref[:, h * dv:(h + 1) * dv] = jnp.dot(lat, wuv_ref[h], preferred_element_type=F32).astype(o_ref.dtype)


def _attn_prompt(q, k, vb, w_uv, layer, tq, tk):
    n_batch, n_heads, n_tok, hw = q.shape
    kv_rank = vb.shape[-1]
    dv = w_uv.shape[-1]
    tiles = n_tok // tq
    return pl.pallas_call(
        functools.partial(_attn_prompt_kernel, tq=tq, tk=tk, n_heads=n_heads, dv=dv),
        grid=(n_batch, tiles),
        in_specs=[
            pl.BlockSpec((None, n_heads, tq, hw), lambda b, i: (b, 0, i, 0)),
            _resident((None, n_heads, n_tok, hw), lambda b, i: (b, 0, 0, 0)),
            _resident((None, n_tok, kv_rank), lambda b, i: (b, 0, 0)),
            _resident((None, n_heads, kv_rank, dv), lambda b, i: (layer, 0, 0, 0)),
        ],
        out_specs=pl.BlockSpec((tq, n_heads * dv), lambda b, i: (b * tiles + i, 0)),
        out_shape=jax.ShapeDtypeStruct((n_batch * n_tok, n_heads * dv), BF16),
        scratch_shapes=[
            pltpu.VMEM((n_heads, tq, LANES), F32),
            pltpu.VMEM((n_heads, tq, LANES), F32),
            pltpu.VMEM((n_heads, tq, kv_rank), F32),
        ],
        compiler_params=_params(("arbitrary", "arbitrary")),
        name="attn_prompt",
    )(q, k, vb, w_uv)


def _attn_sample_kernel(pt_ref, trips_ref, wukt_ref, qt_ref, qa_ref, qb_ref, sself_ref, cnew_ref, cc_hbm, ss_hbm,
                        cache_ckv, cache_krt, lat_ref,
                        wext, cbuf, rbuf, sems, m_s, l_s, acc_s, cc_ref, ss_ref, tbl_sems, *, dims):
    (layer, n_req, n_steps, n_sub, pages_per_sub, page, n_heads, dn, dqk) = dims
    n_rows = n_heads * dn
    tok = pages_per_sub * page
    pages_per_step = n_sub * pages_per_sub
    n_total = n_req * n_steps
    pad_rows = wext.shape[0] - n_rows

    def copies(g, slot):
        b = g // n_steps
        c = g % n_steps
        out = []
        for p in range(pages_per_step):
            pid = pt_ref[b, c * pages_per_step + p]
            sub, pp = divmod(p, pages_per_sub)
            out.append(pltpu.make_async_copy(cache_ckv.at[layer, pid],
                                             cbuf.at[slot, sub, pl.ds(pp * page, page)], sems.at[0, slot, sub]))
            out.append(pltpu.make_async_copy(cache_krt.at[layer, pid],
                                             rbuf.at[slot, sub, :, pl.ds(pp * page, page)], sems.at[1, slot, sub]))
        return out

    table_copies = [pltpu.make_async_copy(cc_hbm, cc_ref, tbl_sems.at[0]),
                    pltpu.make_async_copy(ss_hbm, ss_ref, tbl_sems.at[1])]
    for cp in table_copies:
        cp.start()
    wext[0:n_rows, :] = wukt_ref[...]
    m_s[...] = jnp.zeros(m_s.shape, F32)
    l_s[...] = jnp.zeros(l_s.shape, F32)
    acc_s[...] = jnp.zeros(acc_s.shape, F32)
    for g0 in range(DMA_LOOKAHEAD):
        for cp in copies(g0, g0):
            cp.start()
    for cp in table_copies:
        cp.wait()

    def body(g, carry):
        slot = g % DMA_SLOTS
        b = g // n_steps
        c = g % n_steps
        first = c == 0

        for cp in copies(g, slot):
            cp.wait()

        def rows16(x):
            return jnp.concatenate([x, jnp.zeros((pad_rows - n_heads, x.shape[1]), F32)], axis=0).astype(BF16)

        wext[n_rows:n_rows + pad_rows, :] = rows16(qt_ref[b])
        w = wext[...]
        qa = rows16(qa_ref[b])
        qb = rows16(qb_ref[b])
        def rotary(sub):
            krt = rbuf[slot, sub]
            tbl = c * n_sub + sub
            sa = jnp.dot(qa, (krt * cc_ref[tbl]).astype(BF16), preferred_element_type=F32)[0:n_heads]
            sb = jnp.dot(qb, (krt * ss_ref[tbl]).astype(BF16), preferred_element_type=F32)[0:n_heads]
            return sa + sb, jnp.sum(krt * krt, axis=0, keepdims=True)

        def weigh(cb, big, sr, krsq):
            kt = big[0:n_rows].reshape(n_heads, dn, tok)
            ssq = jnp.sum(kt * kt, axis=1)
            sn = big[n_rows:n_rows + n_heads]
            s = (sn + sr) * lax.rsqrt((ssq + krsq) / dqk + EPS)
            mc = jnp.max(s, axis=-1, keepdims=True)
            p = jnp.exp(s - mc)
            lc = jnp.sum(p, axis=-1, keepdims=True)
            p16 = jnp.concatenate([p, jnp.zeros((pad_rows - n_heads, tok), F32)], axis=0).astype(BF16)
            return mc, lc, jnp.dot(p16, cb, preferred_element_type=F32)[0:n_heads]

        subs = range(n_sub)
        rot = [rotary(sub) for sub in subs]
        cbs = [cbuf[slot, sub].astype(BF16) for sub in subs]
        bigs = [_nt_dot(w, cbs[sub]) for sub in subs]

        for cp in copies(jnp.minimum(g + DMA_LOOKAHEAD, n_total - 1), (g + DMA_LOOKAHEAD) % DMA_SLOTS):
            cp.start()

        parts = [weigh(cbs[sub], bigs[sub], *rot[sub]) for sub in subs]

        m_run = jnp.where(first, sself_ref[b], m_s[...])
        l_run = jnp.where(first, 1.0, l_s[...])
        acc = jnp.where(first, jnp.broadcast_to(cnew_ref[b], acc_s.shape), acc_s[...])
        for mc, lc, pv in parts:
            m_new = jnp.maximum(m_run, mc)
            wa = jnp.exp(m_run - m_new)
            wb = jnp.exp(mc - m_new)
            l_run = l_run * wa + lc * wb
            acc = acc * wa + pv * wb
            m_run = m_new
        m_s[...], l_s[...], acc_s[...] = m_run, l_run, acc
        lat_ref[b] = acc * (1.0 / l_run)
        return carry

    lax.fori_loop(0, trips_ref[0], body, 0)
    for g1 in range(n_total, n_total + DMA_LOOKAHEAD):
        for cp in copies(n_total - 1, g1 % DMA_SLOTS):
            cp.wait()


def _attn_sample(page_table, wukt, qt, qa, qb, sself, cnew, cc3, ss3, cache_ckv, cache_krt,
                 layer, n_sub, pages_per_sub, n_heads, dn, dqk):
    n_req, n_pages = page_table.shape
    page = cache_ckv.shape[2]
    kv_rank = cache_ckv.shape[3]
    rope = cache_krt.shape[2]
    tok = pages_per_sub * page
    pages_per_step = n_sub * pages_per_sub
    n_steps = n_pages // pages_per_step
    assert n_req * n_steps >= DMA_LOOKAHEAD
    pad_rows = 2 * SUBLANES * pl.cdiv(n_heads, 2 * SUBLANES)
    dims = (layer, n_req, n_steps, n_sub, pages_per_sub, page, n_heads, dn, dqk)
    full = lambda *shape: _resident(shape, lambda i, pt, trips: (0,) * len(shape))
    grid_spec = pltpu.PrefetchScalarGridSpec(
        num_scalar_prefetch=2,
        grid=(1,),
        in_specs=[
            full(n_heads * dn, kv_rank),
            full(n_req, n_heads, kv_rank),
            full(n_req, n_heads, rope),
            full(n_req, n_heads, rope),
            full(n_req, n_heads, 1),
            full(n_req, 1, kv_rank),
            pl.BlockSpec(memory_space=pl.ANY),
            pl.BlockSpec(memory_space=pl.ANY),
            pl.BlockSpec(memory_space=pl.ANY),
            pl.BlockSpec(memory_space=pl.ANY),
        ],
        out_specs=pl.BlockSpec((n_req, n_heads, kv_rank), lambda i, pt, trips: (0, 0, 0)),
        scratch_shapes=[
            pltpu.VMEM((n_heads * dn + pad_rows, kv_rank), BF16),
            pltpu.VMEM((DMA_SLOTS, n_sub, tok, kv_rank), F32),
            pltpu.VMEM((DMA_SLOTS, n_sub, rope, tok), F32),
            pltpu.SemaphoreType.DMA((2, DMA_SLOTS, n_sub)),
            pltpu.VMEM((n_heads, 1), F32),
            pltpu.VMEM((n_heads, 1), F32),
            pltpu.VMEM((n_heads, kv_rank), F32),
            pltpu.VMEM(cc3.shape, F32),
            pltpu.VMEM(ss3.shape, F32),
            pltpu.SemaphoreType.DMA((2,)),
        ],
    )
    return pl.pallas_call(
        functools.partial(_attn_sample_kernel, dims=dims),
        grid_spec=grid_spec,
        out_shape=jax.ShapeDtypeStruct((n_req, n_heads, kv_rank), F32),
        compiler_params=_params(("arbitrary",)),
        name="attn_sample",
    )(page_table, jnp.full((1,), n_req * n_steps, jnp.int32), wukt, qt, qa, qb, sself, cnew, cc3, ss3,
      cache_ckv, cache_krt)


def _outproj_kernel(mix_ref, ya_ref, x_ref, g1_ref, sh2_ref, sc2_ref, gffn_ref, wo_ref, x1_ref, h2_ref,
                    *, mix_dim):
    y = jnp.dot(mix_ref[...], wo_ref[0:mix_dim, :], preferred_element_type=F32)
    y = y + jnp.dot(ya_ref[...], wo_ref[mix_dim:, :], preferred_element_type=F32)
    x1 = x_ref[...] + g1_ref[...] * y
    x1_ref[...] = x1
    h2_ref[...] = (_rms(x1, gffn_ref[...]) * (1.0 + sc2_ref[...]) + sh2_ref[...]).astype(BF16)


def _outproj_sample_kernel(mix_ref, lat_ref, wuv_ref, x_ref, g1_ref, sh2_ref, sc2_ref, gffn_ref, wo_ref,
                           x1_ref, h2_ref, *, mix_dim, n_heads, dv):
    y = jnp.dot(mix_ref[...], wo_ref[0:mix_dim, :], preferred_element_type=F32)
    for h in range(n_heads):
        ya = jnp.dot(lat_ref[h].astype(BF16), wuv_ref[h], preferred_element_type=F32).astype(BF16)
        y = y + jnp.dot(ya, wo_ref[mix_dim + h * dv:mix_dim + (h + 1) * dv, :], preferred_element_type=F32)
    x1 = x_ref[...] + g1_ref[...] * y
    x1_ref[...] = x1
    h2_ref[...] = (_rms(x1, gffn_ref[...]) * (1.0 + sc2_ref[...]) + sh2_ref[...]).astype(BF16)


def _outproj(mix, ya, x, mod, g_ffn, w_o, layer, tm, rows_per_batch):
    m, d_model = x.shape
    mix_dim = mix.shape[-1]
    att_dim = ya.shape[-1]
    return pl.pallas_call(
        functools.partial(_outproj_kernel, mix_dim=mix_dim),
        grid=(m // tm,),
        in_specs=[
            pl.BlockSpec((tm, mix_dim), lambda i: (i, 0)),
            pl.BlockSpec((tm, att_dim), lambda i: (i, 0)),
            pl.BlockSpec((tm, d_model), lambda i: (i, 0)),
            _mod_spec(mod, layer, 2, tm, rows_per_batch),
            _mod_spec(mod, layer, 3, tm, rows_per_batch),
            _mod_spec(mod, layer, 4, tm, rows_per_batch),
            pl.BlockSpec((None, 1, d_model), lambda i: (layer, 0, 0)),
            _resident((None, mix_dim + att_dim, d_model), lambda i: (layer, 0, 0)),
        ],
        out_specs=[
            pl.BlockSpec((tm, d_model), lambda i: (i, 0)),
            pl.BlockSpec((tm, d_model), lambda i: (i, 0)),
        ],
        out_shape=[
            jax.ShapeDtypeStruct((m, d_model), F32),
            jax.ShapeDtypeStruct((m, d_model), BF16),
        ],
        compiler_params=_params(("arbitrary",)),
        name="outproj",
    )(mix, ya, x, mod, mod, mod, g_ffn, w_o)


def _outproj_sample(mix, lat_t, w_uv, x, mod, g_ffn, w_o, layer):
    m, d_model = x.shape
    mix_dim = mix.shape[-1]
    n_heads, _, kv_rank = lat_t.shape
    dv = w_uv.shape[-1]
    return pl.pallas_call(
        functools.partial(_outproj_sample_kernel, mix_dim=mix_dim, n_heads=n_heads, dv=dv),
        grid=(1,),
        in_specs=[
            pl.BlockSpec((m, mix_dim), lambda i: (0, 0)),
            pl.BlockSpec((n_heads, m, kv_rank), lambda i: (0, 0, 0)),
            pl.BlockSpec((None, n_heads, kv_rank, dv), lambda i: (layer, 0, 0, 0)),
            pl.BlockSpec((m, d_model), lambda i: (0, 0)),
            _mod_spec(mod, layer, 2, m, 1),
            _mod_spec(mod, layer, 3, m, 1),
            _mod_spec(mod, layer, 4, m, 1),
            pl.BlockSpec((None, 1, d_model), lambda i: (layer, 0, 0)),
            _resident((None, mix_dim + n_heads * dv, d_model), lambda i: (layer, 0, 0)),
        ],
        out_specs=[
            pl.BlockSpec((m, d_model), lambda i: (0, 0)),
            pl.BlockSpec((m, d_model), lambda i: (0, 0)),
        ],
        out_shape=[
            jax.ShapeDtypeStruct((m, d_model), F32),
            jax.ShapeDtypeStruct((m, d_model), BF16),
        ],
        compiler_params=_params(("arbitrary",)),
        name="outproj_sample",
    )(mix, lat_t, w_uv, x, mod, mod, mod, g_ffn, w_o)


def _ffn_kernel(h_ref, x1_hbm, g2_ref, w1_ref, w3_ref, w2_ref, o_ref, x1_buf, x1_sem, *, n_split):
    i = pl.program_id(0)
    k = pl.program_id(1)
    tm = x1_buf.shape[0]
    x1_copy = pltpu.make_async_copy(x1_hbm.at[pl.ds(pl.multiple_of(i * tm, tm), tm)], x1_buf, x1_sem)

    @pl.when(k == 0)
    def _():
        x1_copy.start()
        o_ref[...] = jnp.zeros(o_ref.shape, F32)

    h = h_ref[...]
    hf = w1_ref.shape[-1] // n_split
    ab = [(jnp.dot(h, w1_ref[:, t * hf:(t + 1) * hf], preferred_element_type=F32),
           jnp.dot(h, w3_ref[:, t * hf:(t + 1) * hf], preferred_element_type=F32)) for t in range(n_split)]
    for t, (a, b) in enumerate(ab):
        f = (a * jax.nn.sigmoid(a) * b).astype(BF16)
        o_ref[...] += jnp.dot(f, w2_ref[t * hf:(t + 1) * hf, :], preferred_element_type=F32)

    @pl.when(k == pl.num_programs(1) - 1)
    def _():
        x1_copy.wait()
        o_ref[...] = x1_buf[...] + g2_ref[...] * o_ref[...]


def _ffn(h2, x1, mod, w1, w3, w2, layer, tm, fc, rows_per_batch):
    m, d_model = x1.shape
    d_ff = w1.shape[-1]
    mod_spec = _mod_spec(mod, layer, 5, tm, rows_per_batch)
    mod_map = mod_spec.index_map
    return pl.pallas_call(
        functools.partial(_ffn_kernel, n_split=FFN_SPLIT if fc % (FFN_SPLIT * LANES) == 0 else 1),
        grid=(m // tm, d_ff // fc),
        in_specs=[
            pl.BlockSpec((tm, d_model), lambda i, k: (i, 0)),
            pl.BlockSpec(memory_space=pl.ANY),
            pl.BlockSpec(mod_spec.block_shape, lambda i, k: mod_map(i)),
            pl.BlockSpec((None, d_model, fc), lambda i, k: (layer, 0, k)),
            pl.BlockSpec((None, d_model, fc), lambda i, k: (layer, 0, k)),
            pl.BlockSpec((None, fc, d_model), lambda i, k: (layer, k, 0)),
        ],
        out_specs=pl.BlockSpec((tm, d_model), lambda i, k: (i, 0)),
        out_shape=jax.ShapeDtypeStruct((m, d_model), F32),
        scratch_shapes=[pltpu.VMEM((tm, d_model), F32), pltpu.SemaphoreType.DMA(())],
        compiler_params=_params(("arbitrary", "arbitrary")),
        name="ffn",
    )(h2, x1, mod, w1, w3, w2)


def _spread_last(x, half):
    pad = [(0, 0)] * (x.ndim - 1) + [(0, ROPE_HALF_OFF - half)]
    return jnp.concatenate([jnp.pad(x[..., :half], pad), jnp.pad(x[..., half:], pad)], axis=-1)


def _compact_last(x, half):
    return jnp.concatenate([x[..., :half], x[..., ROPE_HALF_OFF:ROPE_HALF_OFF + half]], axis=-1)


def _rope_tables(pos, half):
    inv = ROPE_THETA ** (-jnp.arange(half, dtype=F32) / half)
    ang = pos.astype(F32)[:, None] * inv[None, :]
    return jnp.cos(ang), jnp.sin(ang)


def kernel(x_prompt, x_sample, cache_ckv, cache_kr, state_conv, state_pool, page_table,
           c_prompt, c_sample, g_mix, w_ada, b_ada, w_in, w_conv, w_pool, s_pool,
           g_cq, g_ckv, w_uq, w_uk, w_uv, g_q, g_k, w_o, g_ffn, w1, w3, w2):
    n_batch, n_tok, d_model = x_prompt.shape
    n_req, dec_seq, _ = x_sample.shape
    assert dec_seq == 1, "the sample group carries one new token per request"
    n_layers = w_in.shape[0]
    conv_w, conv_dim = w_conv.shape[1:]
    pool_dim = s_pool.shape[1]
    pool_ctx = state_pool.shape[2]
    assert pool_ctx == max(POOL_WINDOWS) - 1 and w_pool.shape[1] == len(POOL_WINDOWS)
    q_rank = g_cq.shape[1]
    kv_rank = g_ckv.shape[1]
    n_heads, dqk = w_uq.shape[2:]
    dn = w_uk.shape[3]
    rope = dqk - dn
    half = rope // 2
    dv = w_uv.shape[3]
    page = cache_ckv.shape[2]
    n_pages = page_table.shape[1]
    past = n_pages * page
    base = w_in.shape[2] - rope
    assert base == 3 * conv_dim + pool_dim + q_rank + kv_rank and half <= ROPE_HALF_OFF
    assert dn % LANES == 0 and n_tok >= 2 * SUBLANES
    cfg = (conv_dim, pool_dim, q_rank, kv_rank, n_heads, dn, dqk, conv_w, pool_ctx)

    w_in_b = w_in.astype(BF16)
    w_kr = _spread_last(w_in[..., base:], half).astype(BF16)
    w_uq_p = jnp.concatenate([w_uq[..., :dn], _spread_last(w_uq[..., dn:], half)], axis=-1)
    w_uq_p = w_uq_p.reshape(n_layers, q_rank, n_heads * (dn + ROPE_SPREAD)).astype(BF16)
    g_q_p = jnp.concatenate([g_q[:, :dn], _spread_last(g_q[:, dn:], half)], axis=-1)[:, None, :]
    g_kr = _spread_last(g_k[:, dn:], half)[:, None, :]
    g_krb = _spread_last(jnp.concatenate([g_k[:, dn:dn + half], -g_k[:, dn + half:]], axis=-1), half)[:, None, :]
    w_uk_b = w_uk.reshape(n_layers, kv_rank, n_heads * dn).astype(BF16)
    lw = {
        "w_conv": w_conv, "w_pool": w_pool.astype(BF16), "s_pool": s_pool[:, None, :],
        "g_cq": g_cq[:, None, :], "g_ckv": g_ckv[:, None, :], "w_uq": w_uq_p, "w_uk": w_uk_b,
        "g_q": g_q_p, "g_kn": g_k[:, None, :dn], "g_kr": g_kr, "g_krb": g_krb,
    }
    w_ukt = jnp.swapaxes(w_uk_b, 1, 2)
    w_uv_b = jnp.transpose(w_uv, (0, 2, 1, 3)).astype(BF16)
    w_o_b = w_o.astype(BF16)
    w1_b, w3_b, w2_b = w1.astype(BF16), w3.astype(BF16), w2.astype(BF16)
    g_mix3, g_ffn3 = g_mix[:, None, :], g_ffn[:, None, :]

    cos_p, sin_p = _rope_tables(jnp.arange(n_tok), half)
    cc_p = _spread_last(jnp.concatenate([cos_p, cos_p], axis=-1), half)
    ss_p = _spread_last(jnp.concatenate([-sin_p, sin_p], axis=-1), half)
    cos_s, sin_s = _rope_tables(past + jnp.arange(dec_seq), half)
    cc_s = _spread_last(jnp.concatenate([cos_s, cos_s], axis=-1), half)
    ss_s = _spread_last(jnp.concatenate([-sin_s, sin_s], axis=-1), half)
    tm_p = math.gcd(n_tok, 512)
    tm_ffn = math.gcd(n_tok, 1024)
    tc_p = math.gcd(n_tok, 512)
    tq_p = math.gcd(n_tok, 256)
    tk_p = 2 * tq_p if n_tok % (2 * tq_p) == 0 else tq_p
    fc = math.gcd(w1.shape[2], 512)
    pages_per_sub = math.gcd(n_pages, 4)
    n_sub = math.gcd(n_pages // pages_per_sub, 16)

    cos_c, sin_c = _rope_tables(jnp.arange(past), half)
    sub_tok = pages_per_sub * page
    by_sub = lambda t: jnp.transpose(t.reshape(past // sub_tok, sub_tok, rope), (0, 2, 1))
    cc_c = by_sub(jnp.concatenate([cos_c, cos_c], axis=-1))
    ss_c = by_sub(jnp.concatenate([sin_c, sin_c], axis=-1))
    cache_krt = jnp.swapaxes(cache_kr, 2, 3)

    mod = _ada_mod(jnp.concatenate([c_prompt, c_sample], axis=0), w_ada, b_ada, math.gcd(d_model, 1024))
    mod_p = mod[:, :, :n_batch, None, :]
    mod_s = mod[:, :, n_batch:, :]

    st_conv = jnp.swapaxes(state_conv, 1, 2)
    st_pool = jnp.swapaxes(state_pool, 1, 2)

    xp = x_prompt.reshape(n_batch * n_tok, d_model)
    xs = x_sample.reshape(n_req, d_model)
    outs = {k: [] for k in ("ckv_p", "kr_p", "conv_p", "pool_p", "ckv_s", "kr_s", "conv_s", "pool_s")}
    for l in range(n_layers):
        z = _inproj(xp, mod_p, g_mix3, w_in_b, w_kr, base, l, tm_p, n_tok)
        mix, q, k, ckvn, vb, kr_sp, cst, pst = _mix_prompt(z, cc_p, ss_p, lw, l, n_batch, n_tok, tc_p, cfg)
        ya = _attn_prompt(q, k, vb, w_uv_b, l, tq_p, tk_p)
        x1, h2 = _outproj(mix, ya, xp, mod_p, g_ffn3, w_o_b, l, tm_p, n_tok)
        xp = _ffn(h2, x1, mod_p, w1_b, w3_b, w2_b, l, tm_ffn, fc, n_tok)
        outs["ckv_p"].append(ckvn)
        outs["kr_p"].append(_compact_last(kr_sp, half))
        outs["conv_p"].append(cst)
        outs["pool_p"].append(pst)

        z = _inproj(xs, mod_s, g_mix3, w_in_b, w_kr, base, l, n_req, 1)
        mix, u_new, ckv_new, qt, qa, qb, sself = _mix_sample(z, st_conv, st_pool, cc_s, ss_s, lw, l, cfg, past)
        by_req = lambda a: jnp.swapaxes(a, 0, 1)
        lat = _attn_sample(page_table, w_ukt[l], by_req(qt), by_req(_compact_last(qa, half)),
                           by_req(_compact_last(qb, half)), sself[:, :, None], ckv_new[:, None, :],
                           cc_c, ss_c, cache_ckv, cache_krt, l, n_sub, pages_per_sub, n_heads, dn, dqk)
        x1, h2 = _outproj_sample(mix, jnp.swapaxes(lat, 0, 1), w_uv_b, xs, mod_s, g_ffn3, w_o_b, l)
        xs = _ffn(h2, x1, mod_s, w1_b, w3_b, w2_b, l, n_req, fc, 1)
        outs["ckv_s"].append(ckv_new[:, None, :])
        outs["kr_s"].append(_compact_last(z[:, base:], half)[:, None, :])
        outs["conv_s"].append(jnp.concatenate([state_conv[l][:, 1:], u_new[:, None, :]], axis=1))
        outs["pool_s"].append(jnp.concatenate([state_pool[l][:, 1:], z[:, None, 3 * conv_dim:3 * conv_dim + pool_dim]], axis=1))

    st = {k: jnp.stack(v) for k, v in outs.items()}
    return (xp.reshape(n_batch, n_tok, d_model), xs.reshape(n_req, dec_seq, d_model),
            st["ckv_p"], st["kr_p"], st["conv_p"], st["pool_p"],
            st["ckv_s"], st["kr_s"], st["conv_s"], st["pool_s"])
```
